```python
import jax, jax.numpy as jnp
from jax import lax
import numpy as np

D_MODEL = 4096
BATCH = 2
SEQ = 8192
DEPTH = 4

N_MIXERS = 3
ATTN_HEAD_DIM = 64
ATTN_Q_HEADS = D_MODEL // ATTN_HEAD_DIM
ATTN_KV_HEADS = 8
ATTN_GROUP = ATTN_Q_HEADS // ATTN_KV_HEADS
ATTN_QKV_DIM = (ATTN_Q_HEADS + 2 * ATTN_KV_HEADS) * ATTN_HEAD_DIM
WINDOW = 128
ATTN_BLOCK = WINDOW
ROPE_THETA = 10000.0
HGRN_EXPAND = 128
HGRN_HEADS = D_MODEL // HGRN_EXPAND
HGRN_DK = HGRN_EXPAND
HGRN_DV = D_MODEL // HGRN_HEADS
HGRN_CHUNK = 64
CONV_WIDTH = 3
D_FF = 11008
MLP_CONV_WIDTH = 3
NORM_EPS = 1e-6

LAYER_KINDS = tuple(i % N_MIXERS for i in range(DEPTH))
N_ATTN = LAYER_KINDS.count(0)
N_HGRN = LAYER_KINDS.count(1)
N_CONV = LAYER_KINDS.count(2)

kernel_name = 'hybrid_swa_hgrn2_shortconv_convglu'


def rms_norm(x, gain):
    xf = x.astype(jnp.float32)
    y = xf * lax.rsqrt(jnp.mean(xf * xf, axis=-1, keepdims=True) + NORM_EPS)
    return (y * gain.astype(jnp.float32)).astype(x.dtype)


def causal_depthwise_conv(x, w, b=None):
    width = w.shape[0]
    S = x.shape[1]
    xp = jnp.pad(x, ((0, 0), (width - 1, 0), (0, 0)))
    y = w[0] * xp[:, 0:S]
    for j in range(1, width):
        y = y + w[j] * xp[:, j:j + S]
    if b is not None:
        y = y + b
    return y


def rope(x, positions):
    half = x.shape[-1] // 2
    inv_freq = ROPE_THETA ** (-jnp.arange(half, dtype=jnp.float32) / half)
    ang = positions.astype(jnp.float32)[..., None] * inv_freq
    cos = jnp.cos(ang)[:, :, None, :]
    sin = jnp.sin(ang)[:, :, None, :]
    xf = x.astype(jnp.float32)
    x1, x2 = xf[..., :half], xf[..., half:]
    out = jnp.concatenate([x1 * cos - x2 * sin, x2 * cos + x1 * sin], axis=-1)
    return out.astype(x.dtype)


def sliding_window_attention(h, positions, w_qkv, b_qkv, q_gain, k_gain, sinks, w_o, b_o):
    Bt, S, _ = h.shape
    hd, Hq, Hkv, G, BLK = ATTN_HEAD_DIM, ATTN_Q_HEADS, ATTN_KV_HEADS, ATTN_GROUP, ATTN_BLOCK
    nb = S // BLK
    qkv = h @ w_qkv + b_qkv
    q, k, v = jnp.split(qkv, [Hq * hd, (Hq + Hkv) * hd], axis=-1)
    q = rope(rms_norm(q.reshape(Bt, S, Hq, hd), q_gain), positions)
    k = rope(rms_norm(k.reshape(Bt, S, Hkv, hd), k_gain), positions)
    v = v.reshape(Bt, S, Hkv, hd)
    q_blocks = q.reshape(Bt, nb, BLK, Hkv, G, hd).transpose(1, 0, 3, 4, 2, 5)

    def band(t):
        tp = jnp.pad(t, ((0, 0), (BLK, 0), (0, 0), (0, 0))).reshape(Bt, nb + 1, BLK, Hkv, hd)
        return jnp.concatenate([tp[:, :-1], tp[:, 1:]], axis=2).transpose(1, 0, 3, 2, 4)

    k_band, v_band = band(k), band(v)
    q_idx = jnp.arange(BLK)[:, None] + BLK
    k_idx = jnp.arange(2 * BLK)[None, :]
    rel = q_idx - k_idx
    band_mask = (rel >= 0) & (rel < WINDOW)
    sink = sinks.astype(jnp.float32).reshape(Hkv, G)[None, :, :, None, None]
    scale = ATTN_HEAD_DIM ** -0.5

    def attend_block(args):
        qb, kb, vb, blk = args
        s = jnp.einsum('bhgqd,bhkd->bhgqk', qb, kb).astype(jnp.float32) * scale
        valid = band_mask & ((blk - 1) * BLK + k_idx >= 0)
        s = jnp.where(valid, s, -jnp.inf)
        m = jnp.maximum(jnp.max(s, axis=-1, keepdims=True), sink)
        p = jnp.exp(s - m)
        denom = jnp.sum(p, axis=-1, keepdims=True) + jnp.exp(sink - m)
        return jnp.einsum('bhgqk,bhkd->bhgqd', (p / denom).astype(vb.dtype), vb)

    o = lax.map(attend_block, (q_blocks, k_band, v_band, jnp.arange(nb)))
    o = o.transpose(1, 0, 4, 2, 3, 5).reshape(Bt, S, Hq * hd)
    return o @ w_o + b_o


def hgrn2_mixer(h, lower_bound, w_in, out_gain, w_o):
    Bt, S, _ = h.shape
    H, DK, DV, C = HGRN_HEADS, HGRN_DK, HGRN_DV, HGRN_CHUNK
    nc = S // C
    qfig = h @ w_in
    q, f, i, g = jnp.split(qfig, 4, axis=-1)
    q = jax.nn.silu(q.astype(jnp.float32))
    ff = f.astype(jnp.float32)
    lb = lower_bound.astype(jnp.float32)
    log_forget = jnp.log(lb + (1.0 - lb) * jax.nn.sigmoid(ff))
    k = (1.0 - lb) * jax.nn.sigmoid(-ff)

    def to_chunks(t, d):
        return t.reshape(Bt, nc, C, H, d).transpose(1, 0, 3, 2, 4)

    qc, kc, gc = to_chunks(q, DK), to_chunks(k, DK), to_chunks(log_forget, DK)
    vc = to_chunks(i.astype(jnp.float32), DV)
    causal = jnp.tril(jnp.ones((C, C), dtype=bool))[:, :, None]

    def chunk_step(state, inp):
        qb, kb, vb, gb = inp
        b = jnp.cumsum(gb, axis=2)
        diff = b[:, :, :, None, :] - b[:, :, None, :, :]
        decay = jnp.exp(jnp.where(causal, diff, -jnp.inf))
        scores = jnp.einsum('bhtk,bhsk,bhtsk->bhts', qb, kb, decay)
        o = jnp.einsum('bhts,bhsv->bhtv', scores, vb) + jnp.einsum('bhtk,bhkv->bhtv', qb * jnp.exp(b), state)
        b_last = b[:, :, -1:, :]
        state = jnp.exp(b_last[:, :, 0, :])[..., None] * state + jnp.einsum('bhsk,bhsv->bhkv', kb * jnp.exp(b_last - b), vb)
        return state, o

    state0 = jnp.zeros((Bt, H, DK, DV), jnp.float32)
    _, o = lax.scan(chunk_step, state0, (qc, kc, vc, gc))
    o = o.transpose(1, 0, 3, 2, 4).reshape(Bt, S, H, DV).astype(h.dtype)
    o = rms_norm(o, out_gain) * jax.nn.silu(g.reshape(Bt, S, H, DV))
    return o.reshape(Bt, S, H * DV) @ w_o


def short_conv_mixer(h, w_in, conv_w, w_out):
    bcu = h @ w_in
    b_gate, c_gate, u = jnp.split(bcu, 3, axis=-1)
    y = b_gate * causal_depthwise_conv(c_gate * u, conv_w)
    return y @ w_out


def conv_glu(h, w_up, conv_w, conv_b, w_down):
    gu = h @ w_up
    g, u = jnp.split(gu, 2, axis=-1)
    g = causal_depthwise_conv(g, conv_w, conv_b)
    return (jax.nn.silu(g) * u) @ w_down


def setup_inputs(seed: int = 0) -> dict:
    key = jax.random.key(seed)
    ks = jax.random.split(key, 24)
    D = D_MODEL
    nrm = jax.random.normal
    f32 = jnp.float32
    x = nrm(ks[0], (BATCH, SEQ, D), f32)
    offset = jax.random.randint(ks[1], (BATCH, 1), 0, 1024, dtype=jnp.int32)
    positions = offset + jnp.arange(SEQ, dtype=jnp.int32)[None, :]
    return {
        'x': x,
        'positions': positions,
        'mixer_norm': 1.0 + 0.02 * nrm(ks[2], (DEPTH, D), f32),
        'mlp_norm': 1.0 + 0.02 * nrm(ks[3], (DEPTH, D), f32),
        'attn_w_qkv': nrm(ks[4], (N_ATTN, D, ATTN_QKV_DIM), f32) * D ** -0.5,
        'attn_b_qkv': 0.02 * nrm(ks[5], (N_ATTN, ATTN_QKV_DIM), f32),
        'attn_q_norm': 1.0 + 0.02 * nrm(ks[6], (N_ATTN, ATTN_HEAD_DIM), f32),
        'attn_k_norm': 1.0 + 0.02 * nrm(ks[7], (N_ATTN, ATTN_HEAD_DIM), f32),
        'attn_sinks': 0.5 * nrm(ks[8], (N_ATTN, ATTN_Q_HEADS), f32),
        'attn_w_o': nrm(ks[9], (N_ATTN, D, D), f32) * D ** -0.5,
        'attn_b_o': 0.02 * nrm(ks[10], (N_ATTN, D), f32),
        'hgrn_lower_bounds': 0.1 * nrm(ks[11], (DEPTH, D), f32),
        'hgrn_w_in': nrm(ks[12], (N_HGRN, D, 4 * D), f32) * D ** -0.5,
        'hgrn_out_norm': 1.0 + 0.02 * nrm(ks[13], (N_HGRN, HGRN_DV), f32),
        'hgrn_w_o': nrm(ks[14], (N_HGRN, D, D), f32) * D ** -0.5,
        'conv_w_in': nrm(ks[15], (N_CONV, D, 3 * D), f32) * D ** -0.5,
        'conv_w': nrm(ks[16], (N_CONV, CONV_WIDTH, D), f32) * CONV_WIDTH ** -0.5,
        'conv_w_out': nrm(ks[17], (N_CONV, D, D), f32) * D ** -0.5,
        'mlp_w_up': nrm(ks[18], (DEPTH, D, 2 * D_FF), f32) * D ** -0.5,
        'mlp_conv_w': nrm(ks[19], (DEPTH, MLP_CONV_WIDTH, D_FF), f32) * MLP_CONV_WIDTH ** -0.5,
        'mlp_conv_b': 0.02 * nrm(ks[20], (DEPTH, D_FF), f32),
        'mlp_w_down': nrm(ks[21], (DEPTH, D_FF, D), f32) * D_FF ** -0.5,
    }


def reference(x, positions, mixer_norm, mlp_norm, attn_w_qkv, attn_b_qkv, attn_q_norm, attn_k_norm, attn_sinks, attn_w_o, attn_b_o, hgrn_lower_bounds, hgrn_w_in, hgrn_out_norm, hgrn_w_o, conv_w_in, conv_w, conv_w_out, mlp_w_up, mlp_conv_w, mlp_conv_b, mlp_w_down):
    lb_soft = jax.nn.softmax(hgrn_lower_bounds.astype(jnp.float32), axis=0)
    lb_table = jnp.cumsum(lb_soft, axis=0) - lb_soft[0:1]
    for layer in range(DEPTH):
        kind = LAYER_KINDS[layer]
        j = LAYER_KINDS[:layer].count(kind)
        h = rms_norm(x, mixer_norm[layer])
        if kind == 0:
            mix = sliding_window_attention(h, positions, attn_w_qkv[j], attn_b_qkv[j], attn_q_norm[j], attn_k_norm[j], attn_sinks[j], attn_w_o[j], attn_b_o[j])
        elif kind == 1:
            mix = hgrn2_mixer(h, lb_table[layer], hgrn_w_in[j], hgrn_out_norm[j], hgrn_w_o[j])
        else:
            mix = short_conv_mixer(h, conv_w_in[j], conv_w[j], conv_w_out[j])
        x = x + mix.astype(x.dtype)
        x = x + conv_glu(rms_norm(x, mlp_norm[layer]), mlp_w_up[layer], mlp_conv_w[layer], mlp_conv_b[layer], mlp_w_down[layer]).astype(x.dtype)
    return x
```

```python
import functools

import jax
import jax.numpy as jnp
from jax import lax
from jax.experimental import pallas as pl
from jax.experimental.pallas import tpu as pltpu

F32 = jnp.float32
BF16 = jnp.bfloat16

V7X_LANES = 128
V7X_SUBLANES = 8
V7X_VMEM_LIMIT_BYTES = 56 * 1024 * 1024

N_MIXERS = 3
ATTN_HEAD_DIM = 64
ATTN_KV_HEADS = 8
WINDOW = 128
ROPE_THETA = 10000.0
HGRN_HEAD_DIM = 128
HGRN_CHUNK = 64
HGRN_SUB = 16
NORM_EPS = 1e-6
MASK_VALUE = -1e30


def _params(*semantics):
    return pltpu.CompilerParams(dimension_semantics=semantics, vmem_limit_bytes=V7X_VMEM_LIMIT_BYTES)


def _dot(a, b):
    return jnp.dot(a, b, preferred_element_type=F32)


def _dot_nt(a, b):
    return lax.dot_general(a, b, (((1,), (1,)), ((), ())), preferred_element_type=F32)


def _dot_tn(a, b):
    return lax.dot_general(a, b, (((0,), (0,)), ((), ())), preferred_element_type=F32)


def _split3(x):
    hi = x.astype(BF16)
    r1 = x - hi.astype(F32)
    mid = r1.astype(BF16)
    lo = (r1 - mid.astype(F32)).astype(BF16)
    return hi, mid, lo


def _dot_exact_lhs(sel, x):
    hi, mid, lo = _split3(x)
    return _dot(sel, hi) + _dot(sel, mid) + _dot(sel, lo)


def _sigmoid(x):
    return 1.0 / (1.0 + jnp.exp(-x))


def _silu(x):
    return x * _sigmoid(x)


def _rmsnorm_kernel(x_ref, g_ref, o_ref):
    x = x_ref[...]
    ms = jnp.mean(x * x, axis=-1, keepdims=True)
    o_ref[...] = (x * lax.rsqrt(ms + NORM_EPS) * g_ref[...]).astype(o_ref.dtype)


def rmsnorm(x, gain, *, bm=512):
    M, D = x.shape
    bm = min(bm, M)
    return pl.pallas_call(
        _rmsnorm_kernel,
        grid=(M // bm,),
        in_specs=[pl.BlockSpec((bm, D), lambda i: (i, 0)), pl.BlockSpec((1, D), lambda i: (0, 0))],
        out_specs=pl.BlockSpec((bm, D), lambda i: (i, 0)),
        out_shape=jax.ShapeDtypeStruct((M, D), BF16),
        compiler_params=_params("parallel"),
        name="rmsnorm",
    )(x, gain.reshape(1, D).astype(F32))


def _matmul_kernel(*refs, has_bias, has_resid):
    x_ref, w_ref = refs[0], refs[1]
    o_ref = refs[-1]
    acc = _dot(x_ref[...], w_ref[...])
    k = 2
    if has_bias:
        acc = acc + refs[k][...]
        k += 1
    if has_resid:
        acc = acc + refs[k][...]
    o_ref[...] = acc.astype(o_ref.dtype)


def matmul(x, w, *, bias=None, resid=None, out_dtype=F32, bm=1024, bn=512):
    M, K = x.shape
    N = w.shape[1]
    bm, bn = min(bm, M), min(bn, N)
    in_specs = [pl.BlockSpec((bm, K), lambda n, m: (m, 0)), pl.BlockSpec((K, bn), lambda n, m: (0, n))]
    args = [x, w]
    if bias is not None:
        in_specs.append(pl.BlockSpec((1, bn), lambda n, m: (0, n)))
        args.append(bias.reshape(1, N).astype(F32))
    if resid is not None:
        in_specs.append(pl.BlockSpec((bm, bn), lambda n, m: (m, n)))
        args.append(resid)
    return pl.pallas_call(
        functools.partial(_matmul_kernel, has_bias=bias is not None, has_resid=resid is not None),
        grid=(N // bn, M // bm),
        in_specs=in_specs,
        out_specs=pl.BlockSpec((bm, bn), lambda n, m: (m, n)),
        out_shape=jax.ShapeDtypeStruct((M, N), out_dtype),
        compiler_params=_params("parallel", "parallel"),
        name="matmul",
    )(*args)


def _gated_conv_kernel(*refs, n_branch, bm, tiles_per_seq):
    x_ref = refs[0]
    w_refs = refs[1:1 + n_branch]
    cw_ref, cb_ref, o_ref, zbuf = refs[1 + n_branch:]
    m = pl.program_id(1)
    x = x_ref[...]
    ys = [_dot(x, w[...]) for w in w_refs]
    if n_branch == 2:
        z, gate = ys[0], ys[1]
    else:
        gate, z = ys[0], ys[1] * ys[2]

    first = (m % tiles_per_seq) == 0

    @pl.when(first)
    def _():
        zbuf[0:V7X_SUBLANES, :] = jnp.zeros((V7X_SUBLANES, zbuf.shape[1]), F32)

    @pl.when(jnp.logical_not(first))
    def _():
        zbuf[0:V7X_SUBLANES, :] = zbuf[bm:bm + V7X_SUBLANES, :]

    zbuf[V7X_SUBLANES:V7X_SUBLANES + bm, :] = z
    cw = cw_ref[...]
    conv = (cw[0:1, :] * zbuf[V7X_SUBLANES - 2:V7X_SUBLANES - 2 + bm, :]
            + cw[1:2, :] * zbuf[V7X_SUBLANES - 1:V7X_SUBLANES - 1 + bm, :]
            + cw[2:3, :] * z)
    if n_branch == 2:
        out = _silu(conv + cb_ref[...]) * gate
    else:
        out = gate * conv
    o_ref[...] = out.astype(o_ref.dtype)


def gated_conv_matmul(x, w, conv_w, conv_b, *, n_branch, seq_len, bm=1024, bn=256):
    M, K = x.shape
    N = w.shape[1] // n_branch
    bm, bn = min(bm, seq_len), min(bn, N)
    nb = N // bn
    in_specs = [pl.BlockSpec((bm, K), lambda n, m: (m, 0))]
    for j in range(n_branch):
        in_specs.append(pl.BlockSpec((K, bn), lambda n, m, j=j: (0, n + j * nb)))
    in_specs.append(pl.BlockSpec((3, bn), lambda n, m: (0, n)))
    in_specs.append(pl.BlockSpec((1, bn), lambda n, m: (0, n)))
    if conv_b is None:
        conv_b = jnp.zeros((N,), F32)
    return pl.pallas_call(
        functools.partial(_gated_conv_kernel, n_branch=n_branch, bm=bm, tiles_per_seq=seq_len // bm),
        grid=(nb, M // bm),
        in_specs=in_specs,
        out_specs=pl.BlockSpec((bm, bn), lambda n, m: (m, n)),
        out_shape=jax.ShapeDtypeStruct((M, N), BF16),
        scratch_shapes=[pltpu.VMEM((bm + V7X_SUBLANES, bn), F32)],
        compiler_params=_params("parallel", "arbitrary"),
        name="gated_conv_matmul",
    )(x, *([w] * n_branch), conv_w.astype(F32), conv_b.reshape(1, N).astype(F32))


def _attn_kernel(pos_c_ref, pos_p_ref, q_ref, kc_ref, kp_ref, vc_ref, vp_ref, qg_ref, kg_ref,
                 invf_ref, sgn_ref, sink_ref, o_ref, qs, ks, vs, bias_s, *, n_q_heads, n_kv_heads):
    hd = ATTN_HEAD_DIM
    blk = pl.program_id(1)
    L = V7X_LANES
    group = n_q_heads // n_kv_heads

    lane = lax.broadcasted_iota(jnp.int32, (WINDOW, L), 1)
    first_half = (lane % hd) < (hd // 2)
    gi = lax.broadcasted_iota(jnp.int32, (L, L), 0) // hd
    gj = lax.broadcasted_iota(jnp.int32, (L, L), 1) // hd
    head_mean = jnp.where(gi == gj, 1.0 / hd, 0.0).astype(BF16)

    def tables(pos_ref):
        ang = pos_ref[...] * invf_ref[...]
        return jnp.cos(ang), jnp.sin(ang) * sgn_ref[...]

    def norm_rope(xc, gain, cos, sin, scale):
        hi, mid, lo = _split3(xc * xc)
        ms = _dot(hi, head_mean) + _dot(mid, head_mean) + _dot(lo, head_mean)
        y = xc * lax.rsqrt(ms + NORM_EPS) * gain
        swapped = jnp.where(first_half, pltpu.roll(y, L - hd // 2, axis=1), pltpu.roll(y, hd // 2, axis=1))
        out = y * cos + swapped * sin
        if scale != 1.0:
            out = out * scale
        return out.astype(BF16)

    cos_c, sin_c = tables(pos_c_ref)
    cos_p, sin_p = tables(pos_p_ref)
    qg = qg_ref[...]
    kg = kg_ref[...]
    scale = hd ** -0.5
    for c in range(n_q_heads * hd // L):
        qs[:, c * L:(c + 1) * L] = norm_rope(q_ref[:, c * L:(c + 1) * L], qg, cos_c, sin_c, scale)
    for c in range(n_kv_heads * hd // L):
        sl = slice(c * L, (c + 1) * L)
        ks[0:WINDOW, sl] = norm_rope(kp_ref[:, sl], kg, cos_p, sin_p, 1.0)
        ks[WINDOW:2 * WINDOW, sl] = norm_rope(kc_ref[:, sl], kg, cos_c, sin_c, 1.0)
    vs[0:WINDOW, :] = vp_ref[...].astype(BF16)
    vs[WINDOW:2 * WINDOW, :] = vc_ref[...].astype(BF16)

    qi = lax.broadcasted_iota(jnp.int32, (WINDOW, 2 * WINDOW), 0) + WINDOW
    ki = lax.broadcasted_iota(jnp.int32, (WINDOW, 2 * WINDOW), 1)
    rel = qi - ki
    valid = (rel >= 0) & (rel < WINDOW) & ((ki >= WINDOW) | (blk > 0))
    bias_s[...] = jnp.where(valid, 0.0, MASK_VALUE).astype(F32)

    for hk in range(n_kv_heads):
        kh = ks[:, hk * hd:(hk + 1) * hd]
        vh = vs[:, hk * hd:(hk + 1) * hd]
        for g in range(group):
            hq = hk * group + g
            qh = qs[:, hq * hd:(hq + 1) * hd]
            s = _dot_nt(qh, kh) + bias_s[...]
            sink = sink_ref[hq]
            mx = jnp.maximum(jnp.max(s, axis=1, keepdims=True), sink)
            p = jnp.exp(s - mx)
            denom = jnp.sum(p, axis=1, keepdims=True) + jnp.exp(sink - mx)
            o = _dot(p.astype(BF16), vh) / denom
            o_ref[:, hq * hd:(hq + 1) * hd] = o.astype(o_ref.dtype)


def swa_attention(qkv, pos, q_gain, k_gain, sinks, *, batch, seq_len):
    M = qkv.shape[0]
    hd = ATTN_HEAD_DIM
    n_kv = ATTN_KV_HEADS
    n_q = (qkv.shape[1] - 2 * n_kv * hd) // hd
    dq, dkv = n_q * hd, n_kv * hd
    nb = seq_len // WINDOW
    L = V7X_LANES
    half = hd // 2
    lane = jnp.arange(L)
    inv_freq = ROPE_THETA ** (-(lane % half).astype(F32) / half)
    sgn = jnp.where((lane % hd) < half, -1.0, 1.0).astype(F32)
    kcol, vcol = dq // dkv, dq // dkv + 1

    def cur(b, i):
        return b * nb + i

    def prev(b, i):
        return b * nb + jnp.maximum(i - 1, 0)

    in_specs = [
        pl.BlockSpec((WINDOW, 1), lambda b, i: (cur(b, i), 0)),
        pl.BlockSpec((WINDOW, 1), lambda b, i: (prev(b, i), 0)),
        pl.BlockSpec((WINDOW, dq), lambda b, i: (cur(b, i), 0)),
        pl.BlockSpec((WINDOW, dkv), lambda b, i: (cur(b, i), kcol)),
        pl.BlockSpec((WINDOW, dkv), lambda b, i: (prev(b, i), kcol)),
        pl.BlockSpec((WINDOW, dkv), lambda b, i: (cur(b, i), vcol)),
        pl.BlockSpec((WINDOW, dkv), lambda b, i: (prev(b, i), vcol)),
        pl.BlockSpec((1, L), lambda b, i: (0, 0)),
        pl.BlockSpec((1, L), lambda b, i: (0, 0)),
        pl.BlockSpec((1, L), lambda b, i: (0, 0)),
        pl.BlockSpec((1, L), lambda b, i: (0, 0)),
        pl.BlockSpec(memory_space=pltpu.SMEM),
    ]
    return pl.pallas_call(
        functools.partial(_attn_kernel, n_q_heads=n_q, n_kv_heads=n_kv),
        grid=(batch, nb),
        in_specs=in_specs,
        out_specs=pl.BlockSpec((WINDOW, dq), lambda b, i: (cur(b, i), 0)),
        out_shape=jax.ShapeDtypeStruct((M, dq), BF16),
        scratch_shapes=[
            pltpu.VMEM((WINDOW, dq), BF16),
            pltpu.VMEM((2 * WINDOW, dkv), BF16),
            pltpu.VMEM((2 * WINDOW, dkv), BF16),
            pltpu.VMEM((WINDOW, 2 * WINDOW), F32),
        ],
        compiler_params=_params("parallel", "parallel"),
        name="swa_attention",
    )(pos, pos, qkv, qkv, qkv, qkv, qkv,
      jnp.tile(q_gain.astype(F32), L // hd).reshape(1, L), jnp.tile(k_gain.astype(F32), L // hd).reshape(1, L),
      inv_freq.reshape(1, L), sgn.reshape(1, L), sinks.astype(F32))


def _hgrn_kernel(q_ref, f_ref, i_ref, g_ref, lb_ref, gain_ref, o_ref, st_ref, *, rows):
    C, SUB, DK = HGRN_CHUNK, HGRN_SUB, HGRN_HEAD_DIM
    n_sub = C // SUB
    P = SUB * SUB

    @pl.when(pl.program_id(2) == 0)
    def _():
        st_ref[...] = jnp.zeros(st_ref.shape, F32)

    lb = lb_ref[...]
    one_m_lb = 1.0 - lb
    gain = gain_ref[...]

    ri = lax.broadcasted_iota(jnp.int32, (C, C), 0)
    ci = lax.broadcasted_iota(jnp.int32, (C, C), 1)
    tril = jnp.where(ri >= ci, 1.0, 0.0).astype(BF16)
    pr = lax.broadcasted_iota(jnp.int32, (P, SUB), 0)
    pc = lax.broadcasted_iota(jnp.int32, (P, SUB), 1)
    rep = jnp.where(pr // SUB == pc, 1.0, 0.0).astype(BF16)
    sr = lax.broadcasted_iota(jnp.int32, (SUB, P), 0)
    sc = lax.broadcasted_iota(jnp.int32, (SUB, P), 1)
    seg = jnp.where(sc // SUB == sr, 1.0, 0.0).astype(BF16)
    pt = lax.broadcasted_iota(jnp.int32, (P, DK), 0)
    pair_valid = (pt // SUB) >= (pt % SUB)
    ones_k = jnp.ones((DK, DK), BF16)
    col = lax.broadcasted_iota(jnp.int32, (SUB, C), 1)

    def chunk(c, carry):
        r0 = pl.multiple_of(c * C, C)
        f = f_ref[pl.ds(r0, C), :]
        q = _silu(q_ref[pl.ds(r0, C), :])
        v = i_ref[pl.ds(r0, C), :]
        e = jnp.exp(-jnp.abs(f))
        inv = 1.0 / (1.0 + e)
        sig_pos = jnp.where(f >= 0, inv, e * inv)
        sig_neg = jnp.where(f >= 0, e * inv, inv)
        gk = jnp.log(lb + one_m_lb * sig_pos)
        kk = one_m_lb * sig_neg
        b = _dot_exact_lhs(tril, gk)
        v_bf = v.astype(BF16)
        st = st_ref[...]

        o_inter = _dot_nt((q * jnp.exp(b)).astype(BF16), st.astype(BF16))

        a_rows = [jnp.zeros((SUB, C), F32)]
        for I in range(1, n_sub):
            b_start = b[I * SUB - 1:I * SUB, :]
            qt = q[I * SUB:(I + 1) * SUB, :] * jnp.exp(b[I * SUB:(I + 1) * SUB, :] - b_start)
            kt = kk * jnp.exp(jnp.minimum(b_start - b, 0.0))
            a = _dot_nt(qt.astype(BF16), kt.astype(BF16))
            a_rows.append(jnp.where(col < I * SUB, a, 0.0))
        a_off = jnp.concatenate(a_rows, axis=0)
        o_off = _dot(a_off.astype(BF16), v_bf)

        o_diag = []
        for I in range(n_sub):
            sl = slice(I * SUB, (I + 1) * SUB)
            b_t = _dot_exact_lhs(rep, b[sl, :])
            q_t = _dot_exact_lhs(rep, q[sl, :])
            b_s = jnp.concatenate([b[sl, :]] * SUB, axis=0)
            k_s = jnp.concatenate([kk[sl, :]] * SUB, axis=0)
            v_s = jnp.concatenate([v[sl, :]] * SUB, axis=0)
            w = jnp.exp(jnp.where(pair_valid, b_t - b_s, -jnp.inf)) * (q_t * k_s)
            a_pair = _dot(w.astype(BF16), ones_k)
            o_diag.append(_dot(seg, (a_pair * v_s).astype(BF16)))
        o = o_inter + o_off + jnp.concatenate(o_diag, axis=0)

        b_last = b[C - 1:C, :]
        k_hat = kk * jnp.exp(b_last - b)
        st_ref[...] = st * jnp.exp(b_last) + _dot_tn(v_bf, k_hat.astype(BF16))

        ms = jnp.mean(o * o, axis=-1, keepdims=True)
        y = o * lax.rsqrt(ms + NORM_EPS) * gain * _silu(g_ref[pl.ds(r0, C), :])
        o_ref[pl.ds(r0, C), :] = y.astype(o_ref.dtype)
        return carry

    lax.fori_loop(0, rows // C, chunk, 0)


def hgrn_recurrence(qfig, lb, out_gain, *, batch, seq_len, rows=512):
    M = qfig.shape[0]
    D = qfig.shape[1] // 4
    dk = HGRN_HEAD_DIM
    H = D // dk
    rows = min(rows, seq_len)
    nt = seq_len // rows

    def sect(j):
        return pl.BlockSpec((rows, dk), lambda b, h, t, j=j: (b * nt + t, j * H + h))

    return pl.pallas_call(
        functools.partial(_hgrn_kernel, rows=rows),
        grid=(batch, H, nt),
        in_specs=[sect(0), sect(1), sect(2), sect(3),
                  pl.BlockSpec((1, dk), lambda b, h, t: (0, h)),
                  pl.BlockSpec((1, dk), lambda b, h, t: (0, 0))],
        out_specs=pl.BlockSpec((rows, dk), lambda b, h, t: (b * nt + t, h)),
        out_shape=jax.ShapeDtypeStruct((M, D), BF16),
        scratch_shapes=[pltpu.VMEM((dk, dk), F32)],
        compiler_params=_params("parallel", "parallel", "arbitrary"),
        name="hgrn_recurrence",
    )(qfig, qfig, qfig, qfig, lb.reshape(1, D).astype(F32), out_gain.reshape(1, dk).astype(F32))


def kernel(x, positions, mixer_norm, mlp_norm, attn_w_qkv, attn_b_qkv, attn_q_norm, attn_k_norm, attn_sinks, attn_w_o, attn_b_o, hgrn_lower_bounds, hgrn_w_in, hgrn_out_norm, hgrn_w_o, conv_w_in, conv_w, conv_w_out, mlp_w_up, mlp_conv_w, mlp_conv_b, mlp_w_down):
    B, S, D = x.shape
    M = B * S
    depth = mixer_norm.shape[0]
    xf = x.reshape(M, D).astype(F32)
    pos = positions.reshape(M, 1).astype(F32)
    bf = lambda w: w.astype(BF16)

    lb_soft = jax.nn.softmax(hgrn_lower_bounds.astype(F32), axis=0)
    lb_table = jnp.cumsum(lb_soft, axis=0) - lb_soft[0:1]

    counts = [0] * N_MIXERS
    for layer in range(depth):
        kind = layer % N_MIXERS
        j = counts[kind]
        counts[kind] += 1
        h = rmsnorm(xf, mixer_norm[layer])
        if kind == 0:
            qkv = matmul(h, bf(attn_w_qkv[j]), bias=attn_b_qkv[j])
            o = swa_attention(qkv, pos, attn_q_norm[j], attn_k_norm[j], attn_sinks[j], batch=B, seq_len=S)
            xf = matmul(o, bf(attn_w_o[j]), bias=attn_b_o[j], resid=xf)
        elif kind == 1:
            qfig = matmul(h, bf(hgrn_w_in[j]))
            o = hgrn_recurrence(qfig, lb_table[layer], hgrn_out_norm[j], batch=B, seq_len=S)
            xf = matmul(o, bf(hgrn_w_o[j]), resid=xf)
        else:
            y = gated_conv_matmul(h, bf(conv_w_in[j]), conv_w[j], None, n_branch=3, seq_len=S)
            xf = matmul(y, bf(conv_w_out[j]), resid=xf)
        h = rmsnorm(xf, mlp_norm[layer])
        act = gated_conv_matmul(h, bf(mlp_w_up[layer]), mlp_conv_w[layer], mlp_conv_b[layer], n_branch=2, seq_len=S)
        xf = matmul(act, bf(mlp_w_down[layer]), resid=xf, bm=512, bn=512)
    return xf.reshape(B, S, D).astype(x.dtype)
```

```python
import functools
import math

import jax
import jax.numpy as jnp
from jax import lax
from jax.experimental import pallas as pl
from jax.experimental.pallas import tpu as pltpu

F32 = jnp.float32
BF16 = jnp.bfloat16

V7X_LANES = 128
V7X_SUBLANES = 8
V7X_VMEM_LIMIT_BYTES = 56 * 1024 * 1024
V7X_MXU_ROWS_PER_WEIGHT_PUSH = 128

N_MIXERS = 3
ATTN_HEAD_DIM = 64
ATTN_KV_HEADS = 8
WINDOW = 128
ROPE_THETA = 10000.0
HGRN_HEAD_DIM = 128
HGRN_SUB = 16
NORM_EPS = 1e-6
MASK_VALUE = -1e30
LOG2E = math.log2(math.e)


def _params(*semantics):
    return pltpu.CompilerParams(dimension_semantics=semantics, vmem_limit_bytes=V7X_VMEM_LIMIT_BYTES)


def _dot(a, b):
    return jnp.dot(a, b, preferred_element_type=F32)


def _dot_nt(a, b):
    return lax.dot_general(a, b, (((1,), (1,)), ((), ())), preferred_element_type=F32)


def _dot_tn(a, b):
    return lax.dot_general(a, b, (((0,), (0,)), ((), ())), preferred_element_type=F32)


def _split2(x):
    hi = x.astype(BF16)
    lo = (x - hi.astype(F32)).astype(BF16)
    return hi, lo


def _sigmoid(x):
    return 1.0 / (1.0 + jnp.exp(-x))


def _silu(x):
    return x * _sigmoid(x)


def _with_casts(body, n_in, slab_counts, step_of, n_axes):
    n_cast = len(slab_counts)

    def kernel(*refs):
        cast_in = refs[n_in:n_in + n_cast]
        o_ref = refs[n_in + n_cast]
        cast_out = refs[n_in + n_cast + 1:n_in + 2 * n_cast + 1]
        step = step_of(*[pl.program_id(a) for a in range(n_axes)])
        for src, dst, n_slabs in zip(cast_in, cast_out, slab_counts):
            @pl.when(step < n_slabs)
            def _(src=src, dst=dst):
                dst[...] = src[...].astype(BF16)
        body(*refs[:n_in], o_ref, *refs[n_in + 2 * n_cast + 1:])
    return kernel


def _cast_specs(jobs, n_steps, step_of):
    in_specs, out_specs, out_shapes, slab_counts = [], [], [], []
    bf16_rows = 2 * V7X_SUBLANES
    for arr, first, count in jobs:
        _, rows, cols = arr.shape
        r = next(r for r in range(bf16_rows, rows + 1, bf16_rows)
                 if rows % r == 0 and count * (rows // r) <= n_steps)
        per_layer = rows // r

        def slab(*g, per_layer=per_layer, last=count * per_layer - 1):
            return jnp.minimum(step_of(*g), last)

        in_specs.append(pl.BlockSpec(
            (None, r, cols), lambda *g, s=slab, p=per_layer, first=first: (first + s(*g) // p, s(*g) % p, 0)))
        out_specs.append(pl.BlockSpec((None, r, cols), lambda *g, s=slab, p=per_layer: (s(*g) // p, s(*g) % p, 0)))
        out_shapes.append(jax.ShapeDtypeStruct((count, rows, cols), BF16))
        slab_counts.append(count * per_layer)
    return in_specs, out_specs, out_shapes, slab_counts


def _rmsnorm_kernel(x_ref, g_ref, o_ref):
    x = x_ref[...]
    ms = jnp.mean(x * x, axis=-1, keepdims=True)
    o_ref[...] = (x * lax.rsqrt(ms + NORM_EPS) * g_ref[...]).astype(o_ref.dtype)


def rmsnorm(x, gain, *, bm=512):
    M, D = x.shape
    bm = min(bm, M)
    return pl.pallas_call(
        _rmsnorm_kernel,
        grid=(M // bm,),
        in_specs=[pl.BlockSpec((bm, D), lambda i: (i, 0)), pl.BlockSpec((1, D), lambda i: (0, 0))],
        out_specs=pl.BlockSpec((bm, D), lambda i: (i, 0)),
        out_shape=jax.ShapeDtypeStruct((M, D), BF16),
        compiler_params=_params("parallel"),
        name="rmsnorm",
    )(x, gain.reshape(1, D).astype(F32))


def _matmul_kernel(*refs, has_bias, has_resid, cast_w):
    x_ref, w_ref = refs[0], refs[1]
    if cast_w:
        o_ref, w_bf = refs[-2], refs[-1]

        @pl.when(pl.program_id(1) == 0)
        def _():
            w_bf[...] = w_ref[...].astype(BF16)

        w = w_bf[...]
    else:
        o_ref = refs[-1]
        w = w_ref[...]
    acc = _dot(x_ref[...], w)
    k = 2
    if has_bias:
        acc = acc + refs[k][...]
        k += 1
    if has_resid:
        acc = acc + refs[k][...]
    o_ref[...] = acc.astype(o_ref.dtype)


def matmul(x, w, *, layer=None, bias=None, resid=None, out_dtype=F32, bm=1024, bn=512):
    M, K = x.shape
    N = w.shape[-1]
    bm, bn = min(bm, M), min(bn, N)
    cast_w = w.dtype != BF16
    if layer is None:
        w_spec = pl.BlockSpec((K, bn), lambda n, m: (0, n))
    else:
        w_spec = pl.BlockSpec((None, K, bn), lambda n, m: (layer, 0, n))
    in_specs = [pl.BlockSpec((bm, K), lambda n, m: (m, 0)), w_spec]
    args = [x, w]
    if bias is not None:
        in_specs.append(pl.BlockSpec((1, bn), lambda n, m: (0, n)))
        args.append(bias.reshape(1, N).astype(F32))
    if resid is not None:
        in_specs.append(pl.BlockSpec((bm, bn), lambda n, m: (m, n)))
        args.append(resid)
    return pl.pallas_call(
        functools.partial(_matmul_kernel, has_bias=bias is not None, has_resid=resid is not None, cast_w=cast_w),
        grid=(N // bn, M // bm),
        in_specs=in_specs,
        out_specs=pl.BlockSpec((bm, bn), lambda n, m: (m, n)),
        out_shape=jax.ShapeDtypeStruct((M, N), out_dtype),
        scratch_shapes=[pltpu.VMEM((K, bn), BF16)] if cast_w else [],
        compiler_params=_params("parallel", "arbitrary" if cast_w else "parallel"),
        name="matmul",
    )(*args)


def _gated_conv_kernel(*refs, n_branch, bm, n_mt, tiles_per_seq):
    x_ref = refs[0]
    w_refs = refs[1:1 + n_branch]
    cw_ref, cb_ref, o_ref = refs[1 + n_branch:4 + n_branch]
    scratch = refs[4 + n_branch:]
    set_a, set_b, zbuf = scratch[:n_branch], scratch[n_branch:2 * n_branch], scratch[2 * n_branch]
    S8 = V7X_SUBLANES
    t = pl.program_id(0)

    @pl.when(t == 0)
    def _():
        for buf in scratch:
            buf[...] = jnp.zeros(buf.shape, F32)

    def step(src, dst):
        m_prev = jnp.maximum(t - 1, 0) % n_mt
        seq_start = (m_prev % tiles_per_seq) == 0
        tail = jnp.where(seq_start, 0.0, zbuf[...])
        cw = cw_ref[...]
        sub = V7X_MXU_ROWS_PER_WEIGHT_PUSH
        row8 = lax.broadcasted_iota(jnp.int32, (S8, 1), 0)

        def shifted(z, tail, k):
            s = pltpu.roll(z, k, axis=0)
            head = jnp.where(row8 < k, pltpu.roll(tail, k, axis=0), s[0:S8, :])
            return jnp.concatenate([head, s[S8:, :]], axis=0)

        for i in range(bm // sub):
            r = slice(i * sub, (i + 1) * sub)
            xs = x_ref[r, :]
            for w, buf in zip(w_refs, dst):
                buf[r, :] = _dot(xs, w[...])
            ys = [buf[r, :] for buf in src]
            if n_branch == 2:
                z, gate = ys[0], ys[1]
            else:
                gate, z = ys[0], ys[1] * ys[2]
            conv = cw[0:1, :] * shifted(z, tail, 2) + cw[1:2, :] * shifted(z, tail, 1) + cw[2:3, :] * z
            tail = z[sub - S8:sub, :]
            if n_branch == 2:
                out = _silu(conv + cb_ref[...]) * gate
            else:
                out = gate * conv
            o_ref[r, :] = out.astype(o_ref.dtype)
        zbuf[...] = tail

    @pl.when(t % 2 == 0)
    def _():
        step(set_b, set_a)

    @pl.when(t % 2 == 1)
    def _():
        step(set_a, set_b)


def gated_conv_matmul(x, w, conv_w, conv_b, *, layer, n_branch, seq_len, bm=1024, bn=256):
    M, K = x.shape
    N = w.shape[-1] // n_branch
    bm, bn = min(bm, seq_len), min(bn, N)
    nb, n_mt = N // bn, M // bm
    last = nb * n_mt - 1

    def mm_tile(t):
        return jnp.minimum(t, last)

    def ep_tile(t):
        return jnp.maximum(t - 1, 0)

    in_specs = [pl.BlockSpec((bm, K), lambda t: (mm_tile(t) % n_mt, 0))]
    for j in range(n_branch):
        in_specs.append(pl.BlockSpec((None, K, bn), lambda t, j=j: (layer, 0, mm_tile(t) // n_mt + j * nb)))
    in_specs.append(pl.BlockSpec((3, bn), lambda t: (0, ep_tile(t) // n_mt)))
    in_specs.append(pl.BlockSpec((1, bn), lambda t: (0, ep_tile(t) // n_mt)))
    if conv_b is None:
        conv_b = jnp.zeros((N,), F32)
    return pl.pallas_call(
        functools.partial(_gated_conv_kernel, n_branch=n_branch, bm=bm, n_mt=n_mt, tiles_per_seq=seq_len // bm),
        grid=(nb * n_mt + 1,),
        in_specs=in_specs,
        out_specs=pl.BlockSpec((bm, bn), lambda t: (ep_tile(t) % n_mt, ep_tile(t) // n_mt)),
        out_shape=jax.ShapeDtypeStruct((M, N), BF16),
        scratch_shapes=[pltpu.VMEM((bm, bn), F32)] * (2 * n_branch) + [pltpu.VMEM((V7X_SUBLANES, bn), F32)],
        compiler_params=_params("arbitrary"),
        name="gated_conv_matmul",
    )(x, *([w] * n_branch), conv_w.astype(F32), conv_b.reshape(1, N).astype(F32))


def pair_interleave(w, n_heads):
    half = ATTN_HEAD_DIM // 2
    lead = w.shape[:-1]
    w = w.reshape(*lead, n_heads // 2, 2, 2, half)
    return jnp.swapaxes(w, -3, -2).reshape(*lead, n_heads * ATTN_HEAD_DIM)


def _pair_lane_gain(gain):
    half = ATTN_HEAD_DIM // 2
    g = gain.astype(F32)
    return jnp.concatenate([g[:half], g[:half], g[half:], g[half:]]).reshape(1, V7X_LANES)


def _rope_table_kernel(pos_ref, invf_ref, sgn_ref, cos_ref, sin_ref):
    ang = pos_ref[...] * invf_ref[...]
    cos_ref[...] = jnp.cos(ang)
    sin_ref[...] = jnp.sin(ang) * sgn_ref[...]


def rope_tables(pos, *, bm=512):
    M = pos.shape[0]
    L = V7X_LANES
    half = ATTN_HEAD_DIM // 2
    bm = min(bm, M)
    lane = jnp.arange(L)
    inv_freq = (ROPE_THETA ** (-(lane % half).astype(F32) / half)).reshape(1, L)
    sgn = jnp.where(lane < L // 2, -1.0, 1.0).astype(F32).reshape(1, L)
    row = pl.BlockSpec((1, L), lambda i: (0, 0))
    blk = pl.BlockSpec((bm, L), lambda i: (i, 0))
    return pl.pallas_call(
        _rope_table_kernel,
        grid=(M // bm,),
        in_specs=[pl.BlockSpec((bm, 1), lambda i: (i, 0)), row, row],
        out_specs=[blk, blk],
        out_shape=[jax.ShapeDtypeStruct((M, L), F32)] * 2,
        compiler_params=_params("parallel"),
        name="rope_tables",
    )(pos, inv_freq, sgn)


def _attn_kernel(cos_c_ref, sin_c_ref, cos_p_ref, sin_p_ref, q_ref, kc_ref, kp_ref, vc_ref, vp_ref,
                 qg_ref, kg_ref, sink_ref, o_ref, qs, km, vm, *, n_q_heads, n_kv_heads):
    hd = ATTN_HEAD_DIM
    L = V7X_LANES
    W = WINDOW
    blk = pl.program_id(1)
    group = n_q_heads // n_kv_heads
    pairs_per_kv = group // 2

    lane = lax.broadcasted_iota(jnp.int32, (1, L), 1)
    even_lanes = (lane % hd) < (hd // 2)
    low_lanes = lane < hd
    hi_ = lax.broadcasted_iota(jnp.int32, (L, L), 0)
    hj_ = lax.broadcasted_iota(jnp.int32, (L, L), 1)
    same_head = ((hi_ % hd) // (hd // 2)) == ((hj_ % hd) // (hd // 2))
    head_mean = jnp.where(same_head, 1.0 / hd, 0.0).astype(BF16)

    def norm_rope(x, cg, sg):
        ms = _dot((x * x).astype(BF16), head_mean)
        rs = lax.rsqrt(ms + NORM_EPS)
        return rs * (x * cg + pltpu.roll(x, hd, axis=1) * sg)

    qscale = (hd ** -0.5) * LOG2E
    qg = qg_ref[...] * qscale
    kg = kg_ref[...]
    qg_r = pltpu.roll(qg, hd, axis=1)
    kg_r = pltpu.roll(kg, hd, axis=1)
    cos_c, sin_c = cos_c_ref[...], sin_c_ref[...]
    cos_p, sin_p = cos_p_ref[...], sin_p_ref[...]
    cq, sq = cos_c * qg, sin_c * qg_r
    ckc, skc = cos_c * kg, sin_c * kg_r
    ckp, skp = cos_p * kg, sin_p * kg_r

    for c in range(n_q_heads * hd // L):
        sl = slice(c * L, (c + 1) * L)
        qs[:, sl] = norm_rope(q_ref[:, sl], cq, sq).astype(BF16)

    for c in range(n_kv_heads * hd // L):
        sl = slice(c * L, (c + 1) * L)
        kk = jnp.concatenate([norm_rope(kp_ref[:, sl], ckp, skp), norm_rope(kc_ref[:, sl], ckc, skc)], axis=0)
        vv = jnp.concatenate([vp_ref[:, sl], vc_ref[:, sl]], axis=0)
        k_dn = pltpu.roll(kk, hd // 2, axis=1)
        k_up = pltpu.roll(kk, L - hd // 2, axis=1)
        v_sw = pltpu.roll(vv, hd, axis=1)
        a, b = 2 * c, 2 * c + 1
        km[a, 0] = jnp.where(even_lanes, kk, 0.0).astype(BF16)
        km[a, 1] = jnp.where(even_lanes, 0.0, k_dn).astype(BF16)
        km[b, 0] = jnp.where(even_lanes, k_up, 0.0).astype(BF16)
        km[b, 1] = jnp.where(even_lanes, 0.0, kk).astype(BF16)
        vm[a, 0, :, 0:L] = jnp.where(low_lanes, vv, 0.0).astype(BF16)
        vm[a, 1, :, 0:L] = jnp.where(low_lanes, 0.0, v_sw).astype(BF16)
        vm[b, 0, :, 0:L] = jnp.where(low_lanes, v_sw, 0.0).astype(BF16)
        vm[b, 1, :, 0:L] = jnp.where(low_lanes, 0.0, vv).astype(BF16)
        for h in (a, b):
            for par in range(2):
                vm[h, par, :, L:2 * L] = jnp.ones((2 * W, L), BF16)

    ri = lax.broadcasted_iota(jnp.int32, (W, W), 0)
    ci = lax.broadcasted_iota(jnp.int32, (W, W), 1)
    use_cur = ci <= ri
    prev_bias = jnp.where(blk > 0, 0.0, MASK_VALUE).astype(F32)

    for hk in range(n_kv_heads):
        heads = [(p, par) for p in range(pairs_per_kv) for par in range(2)]
        sinks = [sink_ref[hk * group + 2 * p + par] * LOG2E for p, par in heads]
        scores = []
        for p, par in heads:
            c = hk * pairs_per_kv + p
            s2 = _dot_nt(qs[:, c * L:(c + 1) * L], km[hk, par])
            scores.append(jnp.where(use_cur, s2[:, W:], s2[:, :W] + prev_bias))
        maxes = [jnp.maximum(jnp.max(s, axis=1, keepdims=True), sink) for s, sink in zip(scores, sinks)]
        probs = [jnp.exp2(s - mx) for s, mx in zip(scores, maxes)]
        outs = []
        for (p, par), pr in zip(heads, probs):
            p2 = jnp.concatenate([jnp.where(use_cur, 0.0, pr), jnp.where(use_cur, pr, 0.0)], axis=1)
            outs.append(_dot(p2.astype(BF16), vm[hk, par]))
        for p in range(pairs_per_kv):
            c = hk * pairs_per_kv + p
            (o_a, o_b), (mx_a, mx_b), (sk_a, sk_b) = outs[2 * p:2 * p + 2], maxes[2 * p:2 * p + 2], sinks[2 * p:2 * p + 2]
            den = jnp.where(low_lanes, o_a[:, L:] + jnp.exp2(sk_a - mx_a), o_b[:, L:] + jnp.exp2(sk_b - mx_b))
            o_ref[:, c * L:(c + 1) * L] = ((o_a[:, :L] + o_b[:, :L]) / den).astype(o_ref.dtype)


def swa_attention(qkv, cos_t, sin_t, q_gain, k_gain, sinks, *, batch, seq_len, cast_jobs=()):
    M = qkv.shape[0]
    hd = ATTN_HEAD_DIM
    n_kv = ATTN_KV_HEADS
    n_q = (qkv.shape[1] - 2 * n_kv * hd) // hd
    dq, dkv = n_q * hd, n_kv * hd
    nb = seq_len // WINDOW
    L = V7X_LANES
    kcol, vcol = dq // dkv, dq // dkv + 1

    def cur(b, i):
        return b * nb + i

    def prev(b, i):
        return b * nb + jnp.maximum(i - 1, 0)

    row = pl.BlockSpec((1, L), lambda b, i: (0, 0))
    in_specs = [
        pl.BlockSpec((WINDOW, L), lambda b, i: (cur(b, i), 0)),
        pl.BlockSpec((WINDOW, L), lambda b, i: (cur(b, i), 0)),
        pl.BlockSpec((WINDOW, L), lambda b, i: (prev(b, i), 0)),
        pl.BlockSpec((WINDOW, L), lambda b, i: (prev(b, i), 0)),
        pl.BlockSpec((WINDOW, dq), lambda b, i: (cur(b, i), 0)),
        pl.BlockSpec((WINDOW, dkv), lambda b, i: (cur(b, i), kcol)),
        pl.BlockSpec((WINDOW, dkv), lambda b, i: (prev(b, i), kcol)),
        pl.BlockSpec((WINDOW, dkv), lambda b, i: (cur(b, i), vcol)),
        pl.BlockSpec((WINDOW, dkv), lambda b, i: (prev(b, i), vcol)),
        row, row,
        pl.BlockSpec(memory_space=pltpu.SMEM),
    ]
    c_in, c_out, c_shape, c_slabs = _cast_specs(cast_jobs, batch * nb, cur)
    body = functools.partial(_attn_kernel, n_q_heads=n_q, n_kv_heads=n_kv)
    return pl.pallas_call(
        _with_casts(body, len(in_specs), c_slabs, cur, 2),
        grid=(batch, nb),
        in_specs=in_specs + c_in,
        out_specs=[pl.BlockSpec((WINDOW, dq), lambda b, i: (cur(b, i), 0))] + c_out,
        out_shape=[jax.ShapeDtypeStruct((M, dq), BF16)] + c_shape,
        scratch_shapes=[
            pltpu.VMEM((WINDOW, dq), BF16),
            pltpu.VMEM((n_kv, 2, 2 * WINDOW, L), BF16),
            pltpu.VMEM((n_kv, 2, 2 * WINDOW, 2 * L), BF16),
        ],
        compiler_params=_params(*(["arbitrary" if cast_jobs else "parallel"] * 2)),
        name="swa_attention",
    )(cos_t, sin_t, cos_t, sin_t, qkv, qkv, qkv, qkv, qkv,
      _pair_lane_gain(q_gain), _pair_lane_gain(k_gain), sinks.astype(F32), *[j[0] for j in cast_jobs])


def _hgrn_kernel(q_ref, f_ref, i_ref, g_ref, lb_ref, gain_ref, o_ref, st_ref, upd_ref, a_ref, hist_ref, *, rows):
    SUB, DK, S8 = HGRN_SUB, HGRN_HEAD_DIM, V7X_SUBLANES
    G = V7X_LANES
    nblk = rows // SUB

    @pl.when(pl.program_id(2) == 0)
    def _():
        st_ref[...] = jnp.zeros(st_ref.shape, F32)

    lb = lb_ref[...]
    one_m_lb = 1.0 - lb

    ri = lax.broadcasted_iota(jnp.int32, (2 * G, G), 0)
    ci = lax.broadcasted_iota(jnp.int32, (2 * G, G), 1)
    rr = ri % G
    same_blk = (rr // SUB) == (ci // SUB)
    take = ((ri < G) & (rr >= ci)) | ((ri >= G) & (rr < ci))
    cum_op = jnp.where(same_blk & take, 1.0, 0.0).astype(BF16)
    oi = lax.broadcasted_iota(jnp.int32, (2 * DK, 2 * DK), 0) // DK
    oj = lax.broadcasted_iota(jnp.int32, (2 * DK, 2 * DK), 1) // DK
    ones2 = jnp.where(oi == oj, 1.0, 0.0).astype(BF16)

    f = f_ref[...]
    e = jnp.exp(-jnp.abs(f))
    inv = 1.0 / (1.0 + e)
    sig_pos = jnp.where(f >= 0, inv, e * inv)
    sig_neg = jnp.where(f >= 0, e * inv, inv)
    gk2 = jnp.log2(lb + one_m_lb * sig_pos)
    kk = one_m_lb * sig_neg
    b_parts, r_parts = [], []
    for gi in range(rows // G):
        hi, lo = _split2(gk2[gi * G:(gi + 1) * G, :])
        br = _dot(cum_op, jnp.concatenate([hi, lo], axis=1))
        br = br[:, :DK] + br[:, DK:]
        b_parts.append(br[:G])
        r_parts.append(br[G:])
    b2 = jnp.concatenate(b_parts, axis=0)
    r2 = jnp.concatenate(r_parts, axis=0)
    q = _silu(q_ref[...])
    v = i_ref[...]
    q_dec = (q * jnp.exp2(b2)).astype(BF16)
    k_dec = (kk * jnp.exp2(r2)).astype(BF16)
    v_bf = v.astype(BF16)
    c2 = b2 - jnp.log2(kk)

    def halves(x):
        x4 = x.reshape(nblk, 2, S8, DK)
        return x4[:, 0], x4[:, 1]

    q_lo, q_hi = (h.reshape(nblk * S8, DK).astype(BF16) for h in halves(q))
    b_lo, b_hi = halves(b2)
    c_lo, c_hi = halves(c2)
    sub = lax.broadcasted_iota(jnp.int32, (1, S8, DK), 1)

    def weights(qq_bf, bb, cc, valid):
        ex = bb - cc
        if valid is not None:
            ex = jnp.where(valid, ex, -jnp.inf)
        return qq_bf * jnp.exp2(ex.reshape(nblk * S8, DK).astype(BF16))

    def lane_sums(slot, w0, w1):
        a = _dot(jnp.concatenate([w0, w1], axis=1), ones2)
        a_ref[slot] = a[:, :DK]
        a_ref[slot + 1] = a[:, DK:]

    carry = {"st": st_ref[...]}

    def state_slice(i):
        quarter = nblk // 4
        for n in range((i % 4) * quarter, (i % 4 + 1) * quarter):
            sl = slice(n * SUB, (n + 1) * SUB)
            if i < 4:
                upd_ref[n] = _dot_tn(v_bf[sl, :], k_dec[sl, :])
            else:
                st = carry["st"]
                hist_ref[sl, :] = _dot_nt(q_dec[sl, :], st.astype(BF16))
                carry["st"] = st * jnp.exp2(b2[n * SUB + SUB - 1:n * SUB + SUB, :]) + upd_ref[n]

    lane_sums(0, weights(q_lo, b_lo, c_lo, None), weights(q_hi, b_hi, c_hi, None))
    state_slice(0)
    rolled = [c_lo]
    for d in range(1, S8):
        m = sub >= d
        rc_lo, rc_hi = pltpu.roll(c_lo, d, 1), pltpu.roll(c_hi, d, 1)
        lane_sums(2 * d, weights(q_lo, b_lo, rc_lo, m), weights(q_hi, b_hi, jnp.where(m, rc_hi, rc_lo), None))
        rolled.append(rc_lo)
        state_slice(d)
    for d in range(0, S8, 2):
        lane_sums(2 * S8 + d, weights(q_hi, b_hi, rolled[d], (sub >= d) if d else None),
                  weights(q_hi, b_hi, rolled[d + 1], sub >= d + 1))
    st_ref[...] = carry["st"]

    CH = 8
    gain = gain_ref[...]
    for ck in range(nblk // CH):
        half_rows = slice(ck * CH * S8, (ck + 1) * CH * S8)
        full_rows = slice(ck * CH * SUB, (ck + 1) * CH * SUB)
        v4 = i_ref[full_rows, :].reshape(CH, 2, S8, DK)
        vl, vh = v4[:, 0], v4[:, 1]

        def a_of(slot):
            return a_ref[slot, half_rows, :].reshape(CH, S8, DK)

        acc_lo = a_of(0) * vl
        acc_hi = a_of(1) * vh + a_of(2 * S8) * vl
        for d in range(1, S8):
            m = sub >= d
            rl, rh = pltpu.roll(vl, d, 1), pltpu.roll(vh, d, 1)
            acc_lo = acc_lo + a_of(2 * d) * rl
            acc_hi = acc_hi + a_of(2 * d + 1) * jnp.where(m, rh, rl) + a_of(2 * S8 + d) * rl
        o = jnp.concatenate([acc_lo[:, None], acc_hi[:, None]], axis=1).reshape(CH * SUB, DK) + hist_ref[full_rows, :]
        ms = jnp.mean(o * o, axis=-1, keepdims=True)
        y = o * lax.rsqrt(ms + NORM_EPS) * gain * _silu(g_ref[full_rows, :])
        o_ref[full_rows, :] = y.astype(o_ref.dtype)


def hgrn_recurrence(qfig, lb, out_gain, *, batch, seq_len, rows=1024, cast_jobs=()):
    M = qfig.shape[0]
    D = qfig.shape[1] // 4
    dk = HGRN_HEAD_DIM
    H = D // dk
    rows = min(rows, seq_len)
    nt = seq_len // rows

    def sect(j):
        return pl.BlockSpec((rows, dk), lambda b, h, t, j=j: (b * nt + t, j * H + h))

    def flat_step(b, h, t):
        return (b * H + h) * nt + t

    in_specs = [sect(0), sect(1), sect(2), sect(3),
                pl.BlockSpec((1, dk), lambda b, h, t: (0, h)),
                pl.BlockSpec((1, dk), lambda b, h, t: (0, 0))]
    c_in, c_out, c_shape, c_slabs = _cast_specs(cast_jobs, batch * H * nt, flat_step)
    return pl.pallas_call(
        _with_casts(functools.partial(_hgrn_kernel, rows=rows), len(in_specs), c_slabs, flat_step, 3),
        grid=(batch, H, nt),
        in_specs=in_specs + c_in,
        out_specs=[pl.BlockSpec((rows, dk), lambda b, h, t: (b * nt + t, h))] + c_out,
        out_shape=[jax.ShapeDtypeStruct((M, D), BF16)] + c_shape,
        scratch_shapes=[pltpu.VMEM((dk, dk), F32), pltpu.VMEM((rows // HGRN_SUB, dk, dk), F32),
                        pltpu.VMEM((3 * V7X_SUBLANES, rows // 2, dk), F32), pltpu.VMEM((rows, dk), F32)],
        compiler_params=_params(*(["arbitrary" if cast_jobs else "parallel"] * 2), "arbitrary"),
        name="hgrn_recurrence",
    )(qfig, qfig, qfig, qfig, lb.reshape(1, D).astype(F32), out_gain.reshape(1, dk).astype(F32),
      *[j[0] for j in cast_jobs])


def kernel(x, positions, mixer_norm, mlp_norm, attn_w_qkv, attn_b_qkv, attn_q_norm, attn_k_norm, attn_sinks, attn_w_o, attn_b_o, hgrn_lower_bounds, hgrn_w_in, hgrn_out_norm, hgrn_w_o, conv_w_in, conv_w, conv_w_out, mlp_w_up, mlp_conv_w, mlp_conv_b, mlp_w_down):
    B, S, D = x.shape
    M = B * S
    depth = mixer_norm.shape[0]
    xf = x.reshape(M, D).astype(F32)
    bf = lambda w: w.astype(BF16)

    lb_soft = jax.nn.softmax(hgrn_lower_bounds.astype(F32), axis=0)
    lb_table = jnp.cumsum(lb_soft, axis=0) - lb_soft[0:1]
    cos_t, sin_t = rope_tables(positions.reshape(M, 1).astype(F32))
    dq = attn_w_o.shape[1]
    n_q = dq // ATTN_HEAD_DIM
    dk = dq + ATTN_KV_HEADS * ATTN_HEAD_DIM

    def qkv_layout(w):
        return jnp.concatenate([pair_interleave(w[..., :dq], n_q),
                                pair_interleave(w[..., dq:dk], ATTN_KV_HEADS), w[..., dk:]], axis=-1)

    stacks = {"up": mlp_w_up, "down": mlp_w_down, "cin": conv_w_in, "hin": hgrn_w_in, "hout": hgrn_w_o,
              "cout": conv_w_out, "aout": attn_w_o}
    ready = {}

    def run_with_casts(fn, wanted):
        wanted = [w for w in wanted if w[2] > 0]
        out, *casts = fn(cast_jobs=[(stacks[name], first, count) for name, first, count in wanted])
        for (name, first, count), c in zip(wanted, casts):
            for i in range(count):
                ready[(name, first + i)] = (c, i)
        return out

    def weight(name, l, convert=True):
        if (name, l) not in ready:
            ready[(name, l)] = (bf(stacks[name][l:l + 1]), 0) if convert else (stacks[name], l)
        return ready[(name, l)]

    def out_proj(o, name, l, **kw):
        w, wl = weight(name, l, convert=False)
        return matmul(o, w, layer=wl, bn=1024 if w.dtype == BF16 else 512, **kw)

    counts = [0] * N_MIXERS
    for layer in range(depth):
        kind = layer % N_MIXERS
        j = counts[kind]
        counts[kind] += 1
        h = rmsnorm(xf, mixer_norm[layer])
        if kind == 0:
            qkv = matmul(h, bf(qkv_layout(attn_w_qkv[j])), bias=qkv_layout(attn_b_qkv[j]), bn=1024)
            o = run_with_casts(
                functools.partial(swa_attention, qkv, cos_t, sin_t, attn_q_norm[j], attn_k_norm[j], attn_sinks[j],
                                  batch=B, seq_len=S),
                [("up", 0, 1), ("down", 0, 1), ("hin", 0, hgrn_w_in.shape[0])] if layer == 0 else [])
            xf = out_proj(o, "aout", j, bias=attn_b_o[j], resid=xf)
        elif kind == 1:
            w, wl = weight("hin", j, convert=False)
            qfig = matmul(h, w, layer=wl, bn=1024 if w.dtype == BF16 else 512)
            later = [("up", layer, depth - layer), ("down", layer, depth - layer), ("cin", 0, conv_w_in.shape[0]),
                     ("hout", 0, hgrn_w_o.shape[0]), ("cout", 0, conv_w_out.shape[0]),
                     ("aout", 1, attn_w_o.shape[0] - 1)]
            o = run_with_casts(
                functools.partial(hgrn_recurrence, qfig, lb_table[layer], hgrn_out_norm[j], batch=B, seq_len=S),
                later if j == 0 else [])
            xf = out_proj(o, "hout", j, resid=xf)
        else:
            w, wl = weight("cin", j)
            y = gated_conv_matmul(h, w, conv_w[j], None, layer=wl, n_branch=3, seq_len=S)
            xf = out_proj(y, "cout", j, resid=xf)
        h = rmsnorm(xf, mlp_norm[layer])
        w, wl = weight("up", layer)
        act = gated_conv_matmul(h, w, mlp_conv_w[layer], mlp_conv_b[layer], layer=wl, n_branch=2,
                                seq_len=S, bm=2048)
        w, wl = weight("down", layer)
        xf = matmul(act, w, layer=wl, resid=xf, bm=512, bn=512)
    return xf.reshape(B, S, D).astype(x.dtype)
```

```python
import functools
import math

import jax
import jax.numpy as jnp
from jax import lax
from jax.experimental import pallas as pl
from jax.experimental.pallas import tpu as pltpu

F32 = jnp.float32
BF16 = jnp.bfloat16

V7X_LANES = 128
V7X_SUBLANES = 8
V7X_VMEM_LIMIT_BYTES = 56 * 1024 * 1024
V7X_MXU_ROWS_PER_WEIGHT_PUSH = 128

N_MIXERS = 3
ATTN_HEAD_DIM = 64
ATTN_KV_HEADS = 8
WINDOW = 128
ROPE_THETA = 10000.0
HGRN_HEAD_DIM = 128
HGRN_SUB = 16
NORM_EPS = 1e-6
MASK_VALUE = -1e30
LOG2E = math.log2(math.e)


def _params(*semantics):
    return pltpu.CompilerParams(dimension_semantics=semantics, vmem_limit_bytes=V7X_VMEM_LIMIT_BYTES)


def _dot(a, b):
    return jnp.dot(a, b, preferred_element_type=F32)


def _dot_nt(a, b):
    return lax.dot_general(a, b, (((1,), (1,)), ((), ())), preferred_element_type=F32)


def _dot_tn(a, b):
    return lax.dot_general(a, b, (((0,), (0,)), ((), ())), preferred_element_type=F32)


def _split2(x):
    hi = x.astype(BF16)
    lo = (x - hi.astype(F32)).astype(BF16)
    return hi, lo


def _sigmoid(x):
    return 1.0 / (1.0 + jnp.exp(-x))


def _silu(x):
    return x * _sigmoid(x)


def _with_casts(body, n_in, slab_counts, step_of, n_axes):
    n_cast = len(slab_counts)

    def kernel(*refs):
        cast_in = refs[n_in:n_in + n_cast]
        o_ref = refs[n_in + n_cast]
        cast_out = refs[n_in + n_cast + 1:n_in + 2 * n_cast + 1]
        step = step_of(*[pl.program_id(a) for a in range(n_axes)])
        for src, dst, n_slabs in zip(cast_in, cast_out, slab_counts):
            @pl.when(step < n_slabs)
            def _(src=src, dst=dst):
                dst[...] = src[...].astype(BF16)
        body(*refs[:n_in], o_ref, *refs[n_in + 2 * n_cast + 1:])
    return kernel


def _cast_specs(jobs, n_steps, step_of):
    in_specs, out_specs, out_shapes, slab_counts = [], [], [], []
    bf16_rows = 2 * V7X_SUBLANES
    for arr, first, count in jobs:
        _, rows, cols = arr.shape
        r = next(r for r in range(bf16_rows, rows + 1, bf16_rows)
                 if rows % r == 0 and count * (rows // r) <= n_steps)
        per_layer = rows // r

        def slab(*g, per_layer=per_layer, last=count * per_layer - 1):
            return jnp.minimum(step_of(*g), last)

        in_specs.append(pl.BlockSpec(
            (None, r, cols), lambda *g, s=slab, p=per_layer, first=first: (first + s(*g) // p, s(*g) % p, 0)))
        out_specs.append(pl.BlockSpec((None, r, cols), lambda *g, s=slab, p=per_layer: (s(*g) // p, s(*g) % p, 0)))
        out_shapes.append(jax.ShapeDtypeStruct((count, rows, cols), BF16))
        slab_counts.append(count * per_layer)
    return in_specs, out_specs, out_shapes, slab_counts


def _rmsnorm_kernel(x_ref, g_ref, o_ref, *, chunk):
    g = g_ref[...]
    for c in range(x_ref.shape[0] // chunk):
        rows = slice(c * chunk, (c + 1) * chunk)
        x = x_ref[rows, :]
        ms = jnp.mean(x * x, axis=-1, keepdims=True)
        o_ref[rows, :] = (x * lax.rsqrt(ms + NORM_EPS) * g).astype(o_ref.dtype)


def rmsnorm(x, gain, *, bm=1024, chunk=256):
    M, D = x.shape
    bm = min(bm, M)
    return pl.pallas_call(
        functools.partial(_rmsnorm_kernel, chunk=min(chunk, bm)),
        grid=(M // bm,),
        in_specs=[pl.BlockSpec((bm, D), lambda i: (i, 0)), pl.BlockSpec((1, D), lambda i: (0, 0))],
        out_specs=pl.BlockSpec((bm, D), lambda i: (i, 0)),
        out_shape=jax.ShapeDtypeStruct((M, D), BF16),
        compiler_params=_params("parallel"),
        name="rmsnorm",
    )(x, gain.reshape(1, D).astype(F32))


def _matmul_kernel(*refs, has_bias, has_resid, cast_w):
    x_ref, w_ref = refs[0], refs[1]
    if cast_w:
        o_ref, w_bf = refs[-2], refs[-1]

        @pl.when(pl.program_id(1) == 0)
        def _():
            w_bf[...] = w_ref[...].astype(BF16)

        w = w_bf[...]
    else:
        o_ref = refs[-1]
        w = w_ref[...]
    acc = _dot(x_ref[...], w)
    k = 2
    if has_bias:
        acc = acc + refs[k][...]
        k += 1
    if has_resid:
        acc = acc + refs[k][...]
    o_ref[...] = acc.astype(o_ref.dtype)


def matmul(x, w, *, layer=None, bias=None, resid=None, out_dtype=F32, bm=1024, bn=512):
    M, K = x.shape
    N = w.shape[-1]
    bm, bn = min(bm, M), min(bn, N)
    cast_w = w.dtype != BF16
    if layer is None:
        w_spec = pl.BlockSpec((K, bn), lambda n, m: (0, n))
    else:
        w_spec = pl.BlockSpec((None, K, bn), lambda n, m: (layer, 0, n))
    in_specs = [pl.BlockSpec((bm, K), lambda n, m: (m, 0)), w_spec]
    args = [x, w]
    if bias is not None:
        in_specs.append(pl.BlockSpec((1, bn), lambda n, m: (0, n)))
        args.append(bias.reshape(1, N).astype(F32))
    if resid is not None:
        in_specs.append(pl.BlockSpec((bm, bn), lambda n, m: (m, n)))
        args.append(resid)
    return pl.pallas_call(
        functools.partial(_matmul_kernel, has_bias=bias is not None, has_resid=resid is not None, cast_w=cast_w),
        grid=(N // bn, M // bm),
        in_specs=in_specs,
        out_specs=pl.BlockSpec((bm, bn), lambda n, m: (m, n)),
        out_shape=jax.ShapeDtypeStruct((M, N), out_dtype),
        scratch_shapes=[pltpu.VMEM((K, bn), BF16)] if cast_w else [],
        compiler_params=_params("parallel", "arbitrary" if cast_w else "parallel"),
        name="matmul",
    )(*args)


def _gated_conv_kernel(*refs, n_branch, bm, n_mt, tiles_per_seq):
    x_ref = refs[0]
    w_refs = refs[1:1 + n_branch]
    cw_ref, cb_ref, o_ref = refs[1 + n_branch:4 + n_branch]
    scratch = refs[4 + n_branch:]
    set_a, set_b, zbuf = scratch[:n_branch], scratch[n_branch:2 * n_branch], scratch[2 * n_branch]
    S8 = V7X_SUBLANES
    t = pl.program_id(0)

    @pl.when(t == 0)
    def _():
        for buf in scratch:
            buf[...] = jnp.zeros(buf.shape, F32)

    def step(src, dst):
        m_prev = jnp.maximum(t - 1, 0) % n_mt
        seq_start = (m_prev % tiles_per_seq) == 0
        tail = jnp.where(seq_start, 0.0, zbuf[...])
        cw = cw_ref[...]
        sub = V7X_MXU_ROWS_PER_WEIGHT_PUSH
        row8 = lax.broadcasted_iota(jnp.int32, (S8, 1), 0)

        def shifted(z, tail, k):
            s = pltpu.roll(z, k, axis=0)
            head = jnp.where(row8 < k, pltpu.roll(tail, k, axis=0), s[0:S8, :])
            return jnp.concatenate([head, s[S8:, :]], axis=0)

        for i in range(bm // sub):
            r = slice(i * sub, (i + 1) * sub)
            xs = x_ref[r, :]
            for w, buf in zip(w_refs, dst):
                buf[r, :] = _dot(xs, w[...])
            ys = [buf[r, :] for buf in src]
            if n_branch == 2:
                z, gate = ys[0], ys[1]
            else:
                gate, z = ys[0], ys[1] * ys[2]
            conv = cw[0:1, :] * shifted(z, tail, 2) + cw[1:2, :] * shifted(z, tail, 1) + cw[2:3, :] * z
            tail = z[sub - S8:sub, :]
            if n_branch == 2:
                out = _silu(conv + cb_ref[...]) * gate
            else:
                out = gate * conv
            o_ref[r, :] = out.astype(o_ref.dtype)
        zbuf[...] = tail

    @pl.when(t % 2 == 0)
    def _():
        step(set_b, set_a)

    @pl.when(t % 2 == 1)
    def _():
        step(set_a, set_b)


def gated_conv_matmul(x, w, conv_w, conv_b, *, layer, n_branch, seq_len, bm=1024, bn=256):
    M, K = x.shape
    N = w.shape[-1] // n_branch
    bm, bn = min(bm, seq_len), min(bn, N)
    nb, n_mt = N // bn, M // bm
    last = nb * n_mt - 1

    def mm_tile(t):
        return jnp.minimum(t, last)

    def ep_tile(t):
        return jnp.maximum(t - 1, 0)

    in_specs = [pl.BlockSpec((bm, K), lambda t: (mm_tile(t) % n_mt, 0))]
    for j in range(n_branch):
        in_specs.append(pl.BlockSpec((None, K, bn), lambda t, j=j: (layer, 0, mm_tile(t) // n_mt + j * nb)))
    in_specs.append(pl.BlockSpec((3, bn), lambda t: (0, ep_tile(t) // n_mt)))
    in_specs.append(pl.BlockSpec((1, bn), lambda t: (0, ep_tile(t) // n_mt)))
    if conv_b is None:
        conv_b = jnp.zeros((N,), F32)
    return pl.pallas_call(
        functools.partial(_gated_conv_kernel, n_branch=n_branch, bm=bm, n_mt=n_mt, tiles_per_seq=seq_len // bm),
        grid=(nb * n_mt + 1,),
        in_specs=in_specs,
        out_specs=pl.BlockSpec((bm, bn), lambda t: (ep_tile(t) % n_mt, ep_tile(t) // n_mt)),
        out_shape=jax.ShapeDtypeStruct((M, N), BF16),
        scratch_shapes=[pltpu.VMEM((bm, bn), F32)] * (2 * n_branch) + [pltpu.VMEM((V7X_SUBLANES, bn), F32)],
        compiler_params=_params("arbitrary"),
        name="gated_conv_matmul",
    )(x, *([w] * n_branch), conv_w.astype(F32), conv_b.reshape(1, N).astype(F32))


def pair_interleave(w, n_heads):
    half = ATTN_HEAD_DIM // 2
    lead = w.shape[:-1]
    w = w.reshape(*lead, n_heads // 2, 2, 2, half)
    return jnp.swapaxes(w, -3, -2).reshape(*lead, n_heads * ATTN_HEAD_DIM)


def _pair_lane_gain(gain):
    half = ATTN_HEAD_DIM // 2
    g = gain.astype(F32)
    return jnp.concatenate([g[:half], g[:half], g[half:], g[half:]]).reshape(1, V7X_LANES)


def _rope_table_kernel(pos_ref, invf_ref, sgn_ref, cos_ref, sin_ref):
    ang = pos_ref[...] * invf_ref[...]
    cos_ref[...] = jnp.cos(ang)
    sin_ref[...] = jnp.sin(ang) * sgn_ref[...]


def rope_tables(pos, *, bm=512):
    M = pos.shape[0]
    L = V7X_LANES
    half = ATTN_HEAD_DIM // 2
    bm = min(bm, M)
    lane = jnp.arange(L)
    inv_freq = (ROPE_THETA ** (-(lane % half).astype(F32) / half)).reshape(1, L)
    sgn = jnp.where(lane < L // 2, -1.0, 1.0).astype(F32).reshape(1, L)
    row = pl.BlockSpec((1, L), lambda i: (0, 0))
    blk = pl.BlockSpec((bm, L), lambda i: (i, 0))
    return pl.pallas_call(
        _rope_table_kernel,
        grid=(M // bm,),
        in_specs=[pl.BlockSpec((bm, 1), lambda i: (i, 0)), row, row],
        out_specs=[blk, blk],
        out_shape=[jax.ShapeDtypeStruct((M, L), F32)] * 2,
        compiler_params=_params("parallel"),
        name="rope_tables",
    )(pos, inv_freq, sgn)


def _attn_kernel(cos_c_ref, sin_c_ref, cos_p_ref, sin_p_ref, q_ref, kc_ref, kp_ref, vc_ref, vp_ref,
                 qg_ref, kg_ref, sink_ref, o_ref, qs, km, vm, *, n_q_heads, n_kv_heads):
    hd = ATTN_HEAD_DIM
    L = V7X_LANES
    W = WINDOW
    blk = pl.program_id(1)
    group = n_q_heads // n_kv_heads
    pairs_per_kv = group // 2

    lane = lax.broadcasted_iota(jnp.int32, (1, L), 1)
    even_lanes = (lane % hd) < (hd // 2)
    low_lanes = lane < hd
    hi_ = lax.broadcasted_iota(jnp.int32, (L, L), 0)
    hj_ = lax.broadcasted_iota(jnp.int32, (L, L), 1)
    same_head = ((hi_ % hd) // (hd // 2)) == ((hj_ % hd) // (hd // 2))
    head_mean = jnp.where(same_head, 1.0 / hd, 0.0).astype(BF16)

    def norm_rope(x, cg, sg):
        ms = _dot((x * x).astype(BF16), head_mean)
        rs = lax.rsqrt(ms + NORM_EPS)
        return rs * (x * cg + pltpu.roll(x, hd, axis=1) * sg)

    qscale = (hd ** -0.5) * LOG2E
    qg = qg_ref[...] * qscale
    kg = kg_ref[...]
    qg_r = pltpu.roll(qg, hd, axis=1)
    kg_r = pltpu.roll(kg, hd, axis=1)
    cos_c, sin_c = cos_c_ref[...], sin_c_ref[...]
    cos_p, sin_p = cos_p_ref[...], sin_p_ref[...]
    cq, sq = cos_c * qg, sin_c * qg_r
    ckc, skc = cos_c * kg, sin_c * kg_r
    ckp, skp = cos_p * kg, sin_p * kg_r

    for c in range(n_q_heads * hd // L):
        sl = slice(c * L, (c + 1) * L)
        qs[:, sl] = norm_rope(q_ref[:, sl], cq, sq).astype(BF16)

    for c in range(n_kv_heads * hd // L):
        sl = slice(c * L, (c + 1) * L)
        kk = jnp.concatenate([norm_rope(kp_ref[:, sl], ckp, skp), norm_rope(kc_ref[:, sl], ckc, skc)], axis=0)
        vv = jnp.concatenate([vp_ref[:, sl], vc_ref[:, sl]], axis=0)
        k_dn = pltpu.roll(kk, hd // 2, axis=1)
        k_up = pltpu.roll(kk, L - hd // 2, axis=1)
        v_sw = pltpu.roll(vv, hd, axis=1)
        a, b = 2 * c, 2 * c + 1
        km[a, 0] = jnp.where(even_lanes, kk, 0.0).astype(BF16)
        km[a, 1] = jnp.where(even_lanes, 0.0, k_dn).astype(BF16)
        km[b, 0] = jnp.where(even_lanes, k_up, 0.0).astype(BF16)
        km[b, 1] = jnp.where(even_lanes, 0.0, kk).astype(BF16)
        vm[a, 0, :, 0:L] = jnp.where(low_lanes, vv, 0.0).astype(BF16)
        vm[a, 1, :, 0:L] = jnp.where(low_lanes, 0.0, v_sw).astype(BF16)
        vm[b, 0, :, 0:L] = jnp.where(low_lanes, v_sw, 0.0).astype(BF16)
        vm[b, 1, :, 0:L] = jnp.where(low_lanes, 0.0, vv).astype(BF16)
        for h in (a, b):
            for par in range(2):
                vm[h, par, :, L:2 * L] = jnp.ones((2 * W, L), BF16)

    ri = lax.broadcasted_iota(jnp.int32, (W, W), 0)
    ci = lax.broadcasted_iota(jnp.int32, (W, W), 1)
    use_cur = ci <= ri
    prev_bias = jnp.where(blk > 0, 0.0, MASK_VALUE).astype(F32)

    for hk in range(n_kv_heads):
        heads = [(p, par) for p in range(pairs_per_kv) for par in range(2)]
        sinks = [sink_ref[hk * group + 2 * p + par] * LOG2E for p, par in heads]
        scores = []
        for p, par in heads:
            c = hk * pairs_per_kv + p
            s2 = _dot_nt(qs[:, c * L:(c + 1) * L], km[hk, par])
            scores.append(jnp.where(use_cur, s2[:, W:], s2[:, :W] + prev_bias))
        maxes = [jnp.maximum(jnp.max(s, axis=1, keepdims=True), sink) for s, sink in zip(scores, sinks)]
        probs = [jnp.exp2(s - mx) for s, mx in zip(scores, maxes)]
        outs = []
        for (p, par), pr in zip(heads, probs):
            p2 = jnp.concatenate([jnp.where(use_cur, 0.0, pr), jnp.where(use_cur, pr, 0.0)], axis=1)
            outs.append(_dot(p2.astype(BF16), vm[hk, par]))
        for p in range(pairs_per_kv):
            c = hk * pairs_per_kv + p
            (o_a, o_b), (mx_a, mx_b), (sk_a, sk_b) = outs[2 * p:2 * p + 2], maxes[2 * p:2 * p + 2], sinks[2 * p:2 * p + 2]
            den = jnp.where(low_lanes, o_a[:, L:] + jnp.exp2(sk_a - mx_a), o_b[:, L:] + jnp.exp2(sk_b - mx_b))
            o_ref[:, c * L:(c + 1) * L] = ((o_a[:, :L] + o_b[:, :L]) / den).astype(o_ref.dtype)


def swa_attention(qkv, cos_t, sin_t, q_gain, k_gain, sinks, *, batch, seq_len, cast_jobs=()):
    M = qkv.shape[0]
    hd = ATTN_HEAD_DIM
    n_kv = ATTN_KV_HEADS
    n_q = (qkv.shape[1] - 2 * n_kv * hd) // hd
    dq, dkv = n_q * hd, n_kv * hd
    nb = seq_len // WINDOW
    L = V7X_LANES
    kcol, vcol = dq // dkv, dq // dkv + 1

    def cur(b, i):
        return b * nb + i

    def prev(b, i):
        return b * nb + jnp.maximum(i - 1, 0)

    row = pl.BlockSpec((1, L), lambda b, i: (0, 0))
    in_specs = [
        pl.BlockSpec((WINDOW, L), lambda b, i: (cur(b, i), 0)),
        pl.BlockSpec((WINDOW, L), lambda b, i: (cur(b, i), 0)),
        pl.BlockSpec((WINDOW, L), lambda b, i: (prev(b, i), 0)),
        pl.BlockSpec((WINDOW, L), lambda b, i: (prev(b, i), 0)),
        pl.BlockSpec((WINDOW, dq), lambda b, i: (cur(b, i), 0)),
        pl.BlockSpec((WINDOW, dkv), lambda b, i: (cur(b, i), kcol)),
        pl.BlockSpec((WINDOW, dkv), lambda b, i: (prev(b, i), kcol)),
        pl.BlockSpec((WINDOW, dkv), lambda b, i: (cur(b, i), vcol)),
        pl.BlockSpec((WINDOW, dkv), lambda b, i: (prev(b, i), vcol)),
        row, row,
        pl.BlockSpec(memory_space=pltpu.SMEM),
    ]
    c_in, c_out, c_shape, c_slabs = _cast_specs(cast_jobs, batch * nb, cur)
    body = functools.partial(_attn_kernel, n_q_heads=n_q, n_kv_heads=n_kv)
    return pl.pallas_call(
        _with_casts(body, len(in_specs), c_slabs, cur, 2),
        grid=(batch, nb),
        in_specs=in_specs + c_in,
        out_specs=[pl.BlockSpec((WINDOW, dq), lambda b, i: (cur(b, i), 0))] + c_out,
        out_shape=[jax.ShapeDtypeStruct((M, dq), BF16)] + c_shape,
        scratch_shapes=[
            pltpu.VMEM((WINDOW, dq), BF16),
            pltpu.VMEM((n_kv, 2, 2 * WINDOW, L), BF16),
            pltpu.VMEM((n_kv, 2, 2 * WINDOW, 2 * L), BF16),
        ],
        compiler_params=_params(*(["arbitrary" if cast_jobs else "parallel"] * 2)),
        name="swa_attention",
    )(cos_t, sin_t, cos_t, sin_t, qkv, qkv, qkv, qkv, qkv,
      _pair_lane_gain(q_gain), _pair_lane_gain(k_gain), sinks.astype(F32), *[j[0] for j in cast_jobs])


def _hgrn_kernel(q_ref, f_ref, i_ref, g_ref, lb_ref, gain_ref, o_ref, st_ref, upd_ref, a_ref, hist_ref, *, rows):
    SUB, DK, S8 = HGRN_SUB, HGRN_HEAD_DIM, V7X_SUBLANES
    G = V7X_LANES
    nblk = rows // SUB

    @pl.when(pl.program_id(2) == 0)
    def _():
        st_ref[...] = jnp.zeros(st_ref.shape, F32)

    lb = lb_ref[...]
    one_m_lb = 1.0 - lb

    ri = lax.broadcasted_iota(jnp.int32, (2 * G, G), 0)
    ci = lax.broadcasted_iota(jnp.int32, (2 * G, G), 1)
    rr = ri % G
    same_blk = (rr // SUB) == (ci // SUB)
    take = ((ri < G) & (rr >= ci)) | ((ri >= G) & (rr < ci))
    cum_op = jnp.where(same_blk & take, 1.0, 0.0).astype(BF16)
    oi = lax.broadcasted_iota(jnp.int32, (2 * DK, 2 * DK), 0) // DK
    oj = lax.broadcasted_iota(jnp.int32, (2 * DK, 2 * DK), 1) // DK
    ones2 = jnp.where(oi == oj, 1.0, 0.0).astype(BF16)

    f = f_ref[...]
    e = jnp.exp(-jnp.abs(f))
    inv = 1.0 / (1.0 + e)
    sig_pos = jnp.where(f >= 0, inv, e * inv)
    sig_neg = jnp.where(f >= 0, e * inv, inv)
    gk2 = jnp.log2(lb + one_m_lb * sig_pos)
    kk = one_m_lb * sig_neg
    b_parts, r_parts = [], []
    for gi in range(rows // G):
        hi, lo = _split2(gk2[gi * G:(gi + 1) * G, :])
        br = _dot(cum_op, jnp.concatenate([hi, lo], axis=1))
        br = br[:, :DK] + br[:, DK:]
        b_parts.append(br[:G])
        r_parts.append(br[G:])
    b2 = jnp.concatenate(b_parts, axis=0)
    r2 = jnp.concatenate(r_parts, axis=0)
    q = _silu(q_ref[...])
    v = i_ref[...]
    q_dec = (q * jnp.exp2(b2)).astype(BF16)
    k_dec = (kk * jnp.exp2(r2)).astype(BF16)
    v_bf = v.astype(BF16)
    c2 = b2 - jnp.log2(kk)

    def halves(x):
        x4 = x.reshape(nblk, 2, S8, DK)
        return x4[:, 0], x4[:, 1]

    q_lo, q_hi = (h.reshape(nblk * S8, DK).astype(BF16) for h in halves(q))
    b_lo, b_hi = halves(b2)
    c_lo, c_hi = halves(c2)
    sub = lax.broadcasted_iota(jnp.int32, (1, S8, DK), 1)

    def weights(qq_bf, bb, cc, valid):
        ex = bb - cc
        if valid is not None:
            ex = jnp.where(valid, ex, -jnp.inf)
        return qq_bf * jnp.exp2(ex.reshape(nblk * S8, DK).astype(BF16))

    def lane_sums(slot, w0, w1):
        a = _dot(jnp.concatenate([w0, w1], axis=1), ones2)
        a_ref[slot] = a[:, :DK]
        a_ref[slot + 1] = a[:, DK:]

    carry = {"st": st_ref[...]}

    def state_slice(i):
        quarter = nblk // 4
        for n in range((i % 4) * quarter, (i % 4 + 1) * quarter):
            sl = slice(n * SUB, (n + 1) * SUB)
            if i < 4:
                upd_ref[n] = _dot_tn(v_bf[sl, :], k_dec[sl, :])
            else:
                st = carry["st"]
                hist_ref[sl, :] = _dot_nt(q_dec[sl, :], st.astype(BF16))
                carry["st"] = st * jnp.exp2(b2[n * SUB + SUB - 1:n * SUB + SUB, :]) + upd_ref[n]

    lane_sums(0, weights(q_lo, b_lo, c_lo, None), weights(q_hi, b_hi, c_hi, None))
    state_slice(0)
    rolled = [c_lo]
    for d in range(1, S8):
        m = sub >= d
        rc_lo, rc_hi = pltpu.roll(c_lo, d, 1), pltpu.roll(c_hi, d, 1)
        lane_sums(2 * d, weights(q_lo, b_lo, rc_lo, m), weights(q_hi, b_hi, jnp.where(m, rc_hi, rc_lo), None))
        rolled.append(rc_lo)
        state_slice(d)
    for d in range(0, S8, 2):
        lane_sums(2 * S8 + d, weights(q_hi, b_hi, rolled[d], (sub >= d) if d else None),
                  weights(q_hi, b_hi, rolled[d + 1], sub >= d + 1))
    st_ref[...] = carry["st"]

    CH = 8
    gain = gain_ref[...]
    for ck in range(nblk // CH):
        half_rows = slice(ck * CH * S8, (ck + 1) * CH * S8)
        full_rows = slice(ck * CH * SUB, (ck + 1) * CH * SUB)
        v4 = i_ref[full_rows, :].reshape(CH, 2, S8, DK)
        vl, vh = v4[:, 0], v4[:, 1]

        def a_of(slot):
            return a_ref[slot, half_rows, :].reshape(CH, S8, DK)

        acc_lo = a_of(0) * vl
        acc_hi = a_of(1) * vh + a_of(2 * S8) * vl
        for d in range(1, S8):
            m = sub >= d
            rl, rh = pltpu.roll(vl, d, 1), pltpu.roll(vh, d, 1)
            acc_lo = acc_lo + a_of(2 * d) * rl
            acc_hi = acc_hi + a_of(2 * d + 1) * jnp.where(m, rh, rl) + a_of(2 * S8 + d) * rl
        o = jnp.concatenate([acc_lo[:, None], acc_hi[:, None]], axis=1).reshape(CH * SUB, DK) + hist_ref[full_rows, :]
        ms = jnp.mean(o * o, axis=-1, keepdims=True)
        y = o * lax.rsqrt(ms + NORM_EPS) * gain * _silu(g_ref[full_rows, :])
        o_ref[full_rows, :] = y.astype(o_ref.dtype)


def hgrn_recurrence(qfig, lb, out_gain, *, batch, seq_len, rows=1024, cast_jobs=()):
    M = qfig.shape[0]
    D = qfig.shape[1] // 4
    dk = HGRN_HEAD_DIM
    H = D // dk
    rows = min(rows, seq_len)
    nt = seq_len // rows

    def sect(j):
        return pl.BlockSpec((rows, dk), lambda b, h, t, j=j: (b * nt + t, j * H + h))

    def flat_step(b, h, t):
        return (b * H + h) * nt + t

    in_specs = [sect(0), sect(1), sect(2), sect(3),
                pl.BlockSpec((1, dk), lambda b, h, t: (0, h)),
                pl.BlockSpec((1, dk), lambda b, h, t: (0, 0))]
    c_in, c_out, c_shape, c_slabs = _cast_specs(cast_jobs, batch * H * nt, flat_step)
    return pl.pallas_call(
        _with_casts(functools.partial(_hgrn_kernel, rows=rows), len(in_specs), c_slabs, flat_step, 3),
        grid=(batch, H, nt),
        in_specs=in_specs + c_in,
        out_specs=[pl.BlockSpec((rows, dk), lambda b, h, t: (b * nt + t, h))] + c_out,
        out_shape=[jax.ShapeDtypeStruct((M, D), BF16)] + c_shape,
        scratch_shapes=[pltpu.VMEM((dk, dk), F32), pltpu.VMEM((rows // HGRN_SUB, dk, dk), F32),
                        pltpu.VMEM((3 * V7X_SUBLANES, rows // 2, dk), F32), pltpu.VMEM((rows, dk), F32)],
        compiler_params=_params(*(["arbitrary" if cast_jobs else "parallel"] * 2), "arbitrary"),
        name="hgrn_recurrence",
    )(qfig, qfig, qfig, qfig, lb.reshape(1, D).astype(F32), out_gain.reshape(1, dk).astype(F32),
      *[j[0] for j in cast_jobs])


def kernel(x, positions, mixer_norm, mlp_norm, attn_w_qkv, attn_b_qkv, attn_q_norm, attn_k_norm, attn_sinks, attn_w_o, attn_b_o, hgrn_lower_bounds, hgrn_w_in, hgrn_out_norm, hgrn_w_o, conv_w_in, conv_w, conv_w_out, mlp_w_up, mlp_conv_w, mlp_conv_b, mlp_w_down):
    B, S, D = x.shape
    M = B * S
    depth = mixer_norm.shape[0]
    xf = x.reshape(M, D).astype(F32)
    bf = lambda w: w.astype(BF16)

    lb_soft = jax.nn.softmax(hgrn_lower_bounds.astype(F32), axis=0)
    lb_table = jnp.cumsum(lb_soft, axis=0) - lb_soft[0:1]
    cos_t, sin_t = rope_tables(positions.reshape(M, 1).astype(F32))
    dq = attn_w_o.shape[1]
    n_q = dq // ATTN_HEAD_DIM
    dk = dq + ATTN_KV_HEADS * ATTN_HEAD_DIM

    def qkv_layout(w):
        return jnp.concatenate([pair_interleave(w[..., :dq], n_q),
                                pair_interleave(w[..., dq:dk], ATTN_KV_HEADS), w[..., dk:]], axis=-1)

    stacks = {"up": mlp_w_up, "down": mlp_w_down, "cin": conv_w_in, "hin": hgrn_w_in, "hout": hgrn_w_o,
              "cout": conv_w_out, "aout": attn_w_o}
    ready = {}
    first_up = min(2, depth)

    def run_with_casts(fn, wanted):
        wanted = [w for w in wanted if w[2] > 0]
        out, *casts = fn(cast_jobs=[(stacks[name], first, count) for name, first, count in wanted])
        for (name, first, count), c in zip(wanted, casts):
            for i in range(count):
                ready[(name, first + i)] = (c, i)
        return out

    def weight(name, l, convert=True):
        if (name, l) not in ready:
            ready[(name, l)] = (bf(stacks[name][l:l + 1]), 0) if convert else (stacks[name], l)
        return ready[(name, l)]

    def out_proj(o, name, l, **kw):
        w, wl = weight(name, l, convert=False)
        return matmul(o, w, layer=wl, bn=1024 if w.dtype == BF16 else 512, **kw)

    counts = [0] * N_MIXERS
    for layer in range(depth):
        kind = layer % N_MIXERS
        j = counts[kind]
        counts[kind] += 1
        h = rmsnorm(xf, mixer_norm[layer])
        if kind == 0:
            qkv = matmul(h, bf(qkv_layout(attn_w_qkv[j])), bias=qkv_layout(attn_b_qkv[j]), bn=1024)
            o = run_with_casts(
                functools.partial(swa_attention, qkv, cos_t, sin_t, attn_q_norm[j], attn_k_norm[j], attn_sinks[j],
                                  batch=B, seq_len=S),
                [("up", 0, first_up), ("down", 0, 1), ("hin", 0, hgrn_w_in.shape[0])] if layer == 0 else [])
            xf = out_proj(o, "aout", j, bias=attn_b_o[j], resid=xf)
        elif kind == 1:
            w, wl = weight("hin", j, convert=False)
            qfig = matmul(h, w, layer=wl, bn=1024 if w.dtype == BF16 else 512)
            later = [("up", first_up, depth - first_up), ("down", layer, depth - layer), ("cin", 0, conv_w_in.shape[0]),
                     ("hout", 0, hgrn_w_o.shape[0]), ("cout", 0, conv_w_out.shape[0]),
                     ("aout", 1, attn_w_o.shape[0] - 1)]
            o = run_with_casts(
                functools.partial(hgrn_recurrence, qfig, lb_table[layer], hgrn_out_norm[j], batch=B, seq_len=S),
                later if j == 0 else [])
            xf = out_proj(o, "hout", j, resid=xf)
        else:
            w, wl = weight("cin", j)
            y = gated_conv_matmul(h, w, conv_w[j], None, layer=wl, n_branch=3, seq_len=S)
            xf = out_proj(y, "cout", j, resid=xf)
        h = rmsnorm(xf, mlp_norm[layer])
        w, wl = weight("up", layer)
        act = gated_conv_matmul(h, w, mlp_conv_w[layer], mlp_conv_b[layer], layer=wl, n_branch=2,
                                seq_len=S, bm=2048)
        w, wl = weight("down", layer)
        xf = matmul(act, w, layer=wl, resid=xf, bm=512, bn=512)
    return xf.reshape(B, S, D).astype(x.dtype)
```

```python
import functools
import math

import jax
import jax.numpy as jnp
from jax import lax
from jax.experimental import pallas as pl
from jax.experimental.pallas import tpu as pltpu

F32 = jnp.float32
BF16 = jnp.bfloat16

V7X_LANES = 128
V7X_SUBLANES = 8
V7X_VMEM_LIMIT_BYTES = 56 * 1024 * 1024
V7X_MXU_ROWS_PER_WEIGHT_PUSH = 128

N_MIXERS = 3
ATTN_HEAD_DIM = 64
ATTN_KV_HEADS = 8
WINDOW = 128
ROPE_THETA = 10000.0
HGRN_HEAD_DIM = 128
HGRN_SUB = 16
NORM_EPS = 1e-6
MASK_VALUE = -1e30
LOG2E = math.log2(math.e)


def _params(*semantics):
    return pltpu.CompilerParams(dimension_semantics=semantics, vmem_limit_bytes=V7X_VMEM_LIMIT_BYTES)


def _dot(a, b):
    return jnp.dot(a, b, preferred_element_type=F32)


def _dot_nt(a, b):
    return lax.dot_general(a, b, (((1,), (1,)), ((), ())), preferred_element_type=F32)


def _dot_tn(a, b):
    return lax.dot_general(a, b, (((0,), (0,)), ((), ())), preferred_element_type=F32)


def _split2(x):
    hi = x.astype(BF16)
    lo = (x - hi.astype(F32)).astype(BF16)
    return hi, lo


def _sigmoid(x):
    return 1.0 / (1.0 + jnp.exp(-x))


def _silu(x):
    return x * _sigmoid(x)


def _with_casts(body, n_in, slab_counts, step_of, n_axes):
    n_cast = len(slab_counts)

    def kernel(*refs):
        cast_in = refs[n_in:n_in + n_cast]
        o_ref = refs[n_in + n_cast]
        cast_out = refs[n_in + n_cast + 1:n_in + 2 * n_cast + 1]
        step = step_of(*[pl.program_id(a) for a in range(n_axes)])
        for src, dst, n_slabs in zip(cast_in, cast_out, slab_counts):
            @pl.when(step < n_slabs)
            def _(src=src, dst=dst):
                dst[...] = src[...].astype(BF16)
        body(*refs[:n_in], o_ref, *refs[n_in + 2 * n_cast + 1:])
    return kernel


def _cast_specs(jobs, n_steps, step_of):
    in_specs, out_specs, out_shapes, slab_counts = [], [], [], []
    bf16_rows = 2 * V7X_SUBLANES
    for arr, first, count in jobs:
        _, rows, cols = arr.shape
        r = next(r for r in range(bf16_rows, rows + 1, bf16_rows)
                 if rows % r == 0 and count * (rows // r) <= n_steps)
        per_layer = rows // r

        def slab(*g, per_layer=per_layer, last=count * per_layer - 1):
            return jnp.minimum(step_of(*g), last)

        in_specs.append(pl.BlockSpec(
            (None, r, cols), lambda *g, s=slab, p=per_layer, first=first: (first + s(*g) // p, s(*g) % p, 0)))
        out_specs.append(pl.BlockSpec((None, r, cols), lambda *g, s=slab, p=per_layer: (s(*g) // p, s(*g) % p, 0)))
        out_shapes.append(jax.ShapeDtypeStruct((count, rows, cols), BF16))
        slab_counts.append(count * per_layer)
    return in_specs, out_specs, out_shapes, slab_counts


def _rmsnorm_kernel(x_ref, g_ref, o_ref, *, chunk):
    g = g_ref[...]
    for c in range(x_ref.shape[0] // chunk):
        rows = slice(c * chunk, (c + 1) * chunk)
        x = x_ref[rows, :]
        ms = jnp.mean(x * x, axis=-1, keepdims=True)
        o_ref[rows, :] = (x * lax.rsqrt(ms + NORM_EPS) * g).astype(o_ref.dtype)


def rmsnorm(x, gain, *, bm=1024, chunk=256):
    M, D = x.shape
    bm = min(bm, M)
    return pl.pallas_call(
        functools.partial(_rmsnorm_kernel, chunk=min(chunk, bm)),
        grid=(M // bm,),
        in_specs=[pl.BlockSpec((bm, D), lambda i: (i, 0)), pl.BlockSpec((1, D), lambda i: (0, 0))],
        out_specs=pl.BlockSpec((bm, D), lambda i: (i, 0)),
        out_shape=jax.ShapeDtypeStruct((M, D), BF16),
        compiler_params=_params("parallel"),
        name="rmsnorm",
    )(x, gain.reshape(1, D).astype(F32))


def _matmul_kernel(*refs, has_bias, has_resid, cast_w):
    x_ref, w_ref = refs[0], refs[1]
    if cast_w:
        o_ref, w_bf = refs[-2], refs[-1]

        @pl.when(pl.program_id(1) == 0)
        def _():
            w_bf[...] = w_ref[...].astype(BF16)

        w = w_bf[...]
    else:
        o_ref = refs[-1]
        w = w_ref[...]
    acc = _dot(x_ref[...], w)
    k = 2
    if has_bias:
        acc = acc + refs[k][...]
        k += 1
    if has_resid:
        acc = acc + refs[k][...]
    o_ref[...] = acc.astype(o_ref.dtype)


def matmul(x, w, *, layer=None, bias=None, resid=None, out_dtype=F32, bm=1024, bn=512):
    M, K = x.shape
    N = w.shape[-1]
    bm, bn = min(bm, M), min(bn, N)
    cast_w = w.dtype != BF16
    if layer is None:
        w_spec = pl.BlockSpec((K, bn), lambda n, m: (0, n))
    else:
        w_spec = pl.BlockSpec((None, K, bn), lambda n, m: (layer, 0, n))
    in_specs = [pl.BlockSpec((bm, K), lambda n, m: (m, 0)), w_spec]
    args = [x, w]
    if bias is not None:
        in_specs.append(pl.BlockSpec((1, bn), lambda n, m: (0, n)))
        args.append(bias.reshape(1, N).astype(F32))
    if resid is not None:
        in_specs.append(pl.BlockSpec((bm, bn), lambda n, m: (m, n)))
        args.append(resid)
    return pl.pallas_call(
        functools.partial(_matmul_kernel, has_bias=bias is not None, has_resid=resid is not None, cast_w=cast_w),
        grid=(N // bn, M // bm),
        in_specs=in_specs,
        out_specs=pl.BlockSpec((bm, bn), lambda n, m: (m, n)),
        out_shape=jax.ShapeDtypeStruct((M, N), out_dtype),
        scratch_shapes=[pltpu.VMEM((K, bn), BF16)] if cast_w else [],
        compiler_params=_params("parallel", "arbitrary" if cast_w else "parallel"),
        name="matmul",
    )(*args)


def _gated_conv_kernel(*refs, n_branch, bm, n_mt, tiles_per_seq):
    x_ref = refs[0]
    w_refs = refs[1:1 + n_branch]
    cw_ref, cb_ref, o_ref = refs[1 + n_branch:4 + n_branch]
    scratch = refs[4 + n_branch:]
    set_a, set_b, zbuf = scratch[0:2], scratch[2:4], scratch[4]
    S8 = V7X_SUBLANES
    t = pl.program_id(0)

    @pl.when(t == 0)
    def _():
        for buf in scratch:
            buf[...] = jnp.zeros(buf.shape, F32)

    def step(src, dst):
        m_prev = jnp.maximum(t - 1, 0) % n_mt
        seq_start = (m_prev % tiles_per_seq) == 0
        tail = jnp.where(seq_start, 0.0, zbuf[...])
        cw = cw_ref[...]
        sub = V7X_MXU_ROWS_PER_WEIGHT_PUSH
        row8 = lax.broadcasted_iota(jnp.int32, (S8, 1), 0)

        def shifted(z, tail, k):
            s = pltpu.roll(z, k, axis=0)
            head = jnp.where(row8 < k, pltpu.roll(tail, k, axis=0), s[0:S8, :])
            return jnp.concatenate([head, s[S8:, :]], axis=0)

        for i in range(bm // sub):
            r = slice(i * sub, (i + 1) * sub)
            xs = x_ref[r, :]
            ys = [_dot(xs, w[...]) for w in w_refs]
            dst[0][r, :], dst[1][r, :] = (ys[0], ys[1]) if n_branch == 2 else (ys[1] * ys[2], ys[0])
            z, gate = src[0][r, :], src[1][r, :]
            conv = cw[0:1, :] * shifted(z, tail, 2) + cw[1:2, :] * shifted(z, tail, 1) + cw[2:3, :] * z
            tail = z[sub - S8:sub, :]
            if n_branch == 2:
                out = _silu(conv + cb_ref[...]) * gate
            else:
                out = gate * conv
            o_ref[r, :] = out.astype(o_ref.dtype)
        zbuf[...] = tail

    @pl.when(t % 2 == 0)
    def _():
        step(set_b, set_a)

    @pl.when(t % 2 == 1)
    def _():
        step(set_a, set_b)


def gated_conv_matmul(x, w, conv_w, conv_b, *, layer, n_branch, seq_len, bm=1024, bn=256):
    M, K = x.shape
    N = w.shape[-1] // n_branch
    bm, bn = min(bm, seq_len), min(bn, N)
    nb, n_mt = N // bn, M // bm
    last = nb * n_mt - 1

    def mm_tile(t):
        return jnp.minimum(t, last)

    def ep_tile(t):
        return jnp.maximum(t - 1, 0)

    in_specs = [pl.BlockSpec((bm, K), lambda t: (mm_tile(t) % n_mt, 0))]
    for j in range(n_branch):
        in_specs.append(pl.BlockSpec((None, K, bn), lambda t, j=j: (layer, 0, mm_tile(t) // n_mt + j * nb)))
    in_specs.append(pl.BlockSpec((3, bn), lambda t: (0, ep_tile(t) // n_mt)))
    in_specs.append(pl.BlockSpec((1, bn), lambda t: (0, ep_tile(t) // n_mt)))
    if conv_b is None:
        conv_b = jnp.zeros((N,), F32)
    return pl.pallas_call(
        functools.partial(_gated_conv_kernel, n_branch=n_branch, bm=bm, n_mt=n_mt, tiles_per_seq=seq_len // bm),
        grid=(nb * n_mt + 1,),
        in_specs=in_specs,
        out_specs=pl.BlockSpec((bm, bn), lambda t: (ep_tile(t) % n_mt, ep_tile(t) // n_mt)),
        out_shape=jax.ShapeDtypeStruct((M, N), BF16),
        scratch_shapes=[pltpu.VMEM((bm, bn), F32)] * 4 + [pltpu.VMEM((V7X_SUBLANES, bn), F32)],
        compiler_params=_params("arbitrary"),
        name="gated_conv_matmul",
    )(x, *([w] * n_branch), conv_w.astype(F32), conv_b.reshape(1, N).astype(F32))


def pair_interleave(w, n_heads):
    half = ATTN_HEAD_DIM // 2
    lead = w.shape[:-1]
    w = w.reshape(*lead, n_heads // 2, 2, 2, half)
    return jnp.swapaxes(w, -3, -2).reshape(*lead, n_heads * ATTN_HEAD_DIM)


def _pair_lane_gain(gain):
    half = ATTN_HEAD_DIM // 2
    g = gain.astype(F32)
    return jnp.concatenate([g[:half], g[:half], g[half:], g[half:]]).reshape(1, V7X_LANES)


def _rope_table_kernel(pos_ref, invf_ref, sgn_ref, cos_ref, sin_ref):
    ang = pos_ref[...] * invf_ref[...]
    cos_ref[...] = jnp.cos(ang)
    sin_ref[...] = jnp.sin(ang) * sgn_ref[...]


def rope_tables(pos, *, bm=512):
    M = pos.shape[0]
    L = V7X_LANES
    half = ATTN_HEAD_DIM // 2
    bm = min(bm, M)
    lane = jnp.arange(L)
    inv_freq = (ROPE_THETA ** (-(lane % half).astype(F32) / half)).reshape(1, L)
    sgn = jnp.where(lane < L // 2, -1.0, 1.0).astype(F32).reshape(1, L)
    row = pl.BlockSpec((1, L), lambda i: (0, 0))
    blk = pl.BlockSpec((bm, L), lambda i: (i, 0))
    return pl.pallas_call(
        _rope_table_kernel,
        grid=(M // bm,),
        in_specs=[pl.BlockSpec((bm, 1), lambda i: (i, 0)), row, row],
        out_specs=[blk, blk],
        out_shape=[jax.ShapeDtypeStruct((M, L), F32)] * 2,
        compiler_params=_params("parallel"),
        name="rope_tables",
    )(pos, inv_freq, sgn)


def _attn_kernel(cos_c_ref, sin_c_ref, cos_p_ref, sin_p_ref, q_ref, kc_ref, kp_ref, vc_ref, vp_ref,
                 qg_ref, kg_ref, sink_ref, o_ref, qs, km, vm, *, n_q_heads, n_kv_heads):
    hd = ATTN_HEAD_DIM
    L = V7X_LANES
    W = WINDOW
    blk = pl.program_id(1)
    group = n_q_heads // n_kv_heads
    pairs_per_kv = group // 2

    lane = lax.broadcasted_iota(jnp.int32, (1, L), 1)
    even_lanes = (lane % hd) < (hd // 2)
    low_lanes = lane < hd
    hi_ = lax.broadcasted_iota(jnp.int32, (L, L), 0)
    hj_ = lax.broadcasted_iota(jnp.int32, (L, L), 1)
    same_head = ((hi_ % hd) // (hd // 2)) == ((hj_ % hd) // (hd // 2))
    head_mean = jnp.where(same_head, 1.0 / hd, 0.0).astype(BF16)

    def norm_rope(x, cg, sg):
        ms = _dot((x * x).astype(BF16), head_mean)
        rs = lax.rsqrt(ms + NORM_EPS)
        return rs * (x * cg + pltpu.roll(x, hd, axis=1) * sg)

    qscale = (hd ** -0.5) * LOG2E
    qg = qg_ref[...] * qscale
    kg = kg_ref[...]
    qg_r = pltpu.roll(qg, hd, axis=1)
    kg_r = pltpu.roll(kg, hd, axis=1)
    cos_c, sin_c = cos_c_ref[...], sin_c_ref[...]
    cos_p, sin_p = cos_p_ref[...], sin_p_ref[...]
    cq, sq = cos_c * qg, sin_c * qg_r
    ckc, skc = cos_c * kg, sin_c * kg_r
    ckp, skp = cos_p * kg, sin_p * kg_r

    for c in range(n_q_heads * hd // L):
        sl = slice(c * L, (c + 1) * L)
        qs[:, sl] = norm_rope(q_ref[:, sl], cq, sq).astype(BF16)

    for c in range(n_kv_heads * hd // L):
        sl = slice(c * L, (c + 1) * L)
        kk = jnp.concatenate([norm_rope(kp_ref[:, sl], ckp, skp), norm_rope(kc_ref[:, sl], ckc, skc)], axis=0)
        vv = jnp.concatenate([vp_ref[:, sl], vc_ref[:, sl]], axis=0)
        k_dn = pltpu.roll(kk, hd // 2, axis=1)
        k_up = pltpu.roll(kk, L - hd // 2, axis=1)
        v_sw = pltpu.roll(vv, hd, axis=1)
        a, b = 2 * c, 2 * c + 1
        km[a, 0] = jnp.where(even_lanes, kk, 0.0).astype(BF16)
        km[a, 1] = jnp.where(even_lanes, 0.0, k_dn).astype(BF16)
        km[b, 0] = jnp.where(even_lanes, k_up, 0.0).astype(BF16)
        km[b, 1] = jnp.where(even_lanes, 0.0, kk).astype(BF16)
        vm[a, 0, :, 0:L] = jnp.where(low_lanes, vv, 0.0).astype(BF16)
        vm[a, 1, :, 0:L] = jnp.where(low_lanes, 0.0, v_sw).astype(BF16)
        vm[b, 0, :, 0:L] = jnp.where(low_lanes, v_sw, 0.0).astype(BF16)
        vm[b, 1, :, 0:L] = jnp.where(low_lanes, 0.0, vv).astype(BF16)
        for h in (a, b):
            for par in range(2):
                vm[h, par, :, L:2 * L] = jnp.ones((2 * W, L), BF16)

    ri = lax.broadcasted_iota(jnp.int32, (W, W), 0)
    ci = lax.broadcasted_iota(jnp.int32, (W, W), 1)
    use_cur = ci <= ri
    prev_bias = jnp.where(blk > 0, 0.0, MASK_VALUE).astype(F32)

    for hk in range(n_kv_heads):
        heads = [(p, par) for p in range(pairs_per_kv) for par in range(2)]
        sinks = [sink_ref[hk * group + 2 * p + par] * LOG2E for p, par in heads]
        scores = []
        for p, par in heads:
            c = hk * pairs_per_kv + p
            s2 = _dot_nt(qs[:, c * L:(c + 1) * L], km[hk, par])
            scores.append(jnp.where(use_cur, s2[:, W:], s2[:, :W] + prev_bias))
        maxes = [jnp.maximum(jnp.max(s, axis=1, keepdims=True), sink) for s, sink in zip(scores, sinks)]
        probs = [jnp.exp2(s - mx) for s, mx in zip(scores, maxes)]
        outs = []
        for (p, par), pr in zip(heads, probs):
            p2 = jnp.concatenate([jnp.where(use_cur, 0.0, pr), jnp.where(use_cur, pr, 0.0)], axis=1)
            outs.append(_dot(p2.astype(BF16), vm[hk, par]))
        for p in range(pairs_per_kv):
            c = hk * pairs_per_kv + p
            (o_a, o_b), (mx_a, mx_b), (sk_a, sk_b) = outs[2 * p:2 * p + 2], maxes[2 * p:2 * p + 2], sinks[2 * p:2 * p + 2]
            den = jnp.where(low_lanes, o_a[:, L:] + jnp.exp2(sk_a - mx_a), o_b[:, L:] + jnp.exp2(sk_b - mx_b))
            o_ref[:, c * L:(c + 1) * L] = ((o_a[:, :L] + o_b[:, :L]) / den).astype(o_ref.dtype)


def swa_attention(qkv, cos_t, sin_t, q_gain, k_gain, sinks, *, batch, seq_len, cast_jobs=()):
    M = qkv.shape[0]
    hd = ATTN_HEAD_DIM
    n_kv = ATTN_KV_HEADS
    n_q = (qkv.shape[1] - 2 * n_kv * hd) // hd
    dq, dkv = n_q * hd, n_kv * hd
    nb = seq_len // WINDOW
    L = V7X_LANES
    kcol, vcol = dq // dkv, dq // dkv + 1

    def cur(b, i):
        return b * nb + i

    def prev(b, i):
        return b * nb + jnp.maximum(i - 1, 0)

    row = pl.BlockSpec((1, L), lambda b, i: (0, 0))
    in_specs = [
        pl.BlockSpec((WINDOW, L), lambda b, i: (cur(b, i), 0)),
        pl.BlockSpec((WINDOW, L), lambda b, i: (cur(b, i), 0)),
        pl.BlockSpec((WINDOW, L), lambda b, i: (prev(b, i), 0)),
        pl.BlockSpec((WINDOW, L), lambda b, i: (prev(b, i), 0)),
        pl.BlockSpec((WINDOW, dq), lambda b, i: (cur(b, i), 0)),
        pl.BlockSpec((WINDOW, dkv), lambda b, i: (cur(b, i), kcol)),
        pl.BlockSpec((WINDOW, dkv), lambda b, i: (prev(b, i), kcol)),
        pl.BlockSpec((WINDOW, dkv), lambda b, i: (cur(b, i), vcol)),
        pl.BlockSpec((WINDOW, dkv), lambda b, i: (prev(b, i), vcol)),
        row, row,
        pl.BlockSpec(memory_space=pltpu.SMEM),
    ]
    c_in, c_out, c_shape, c_slabs = _cast_specs(cast_jobs, batch * nb, cur)
    body = functools.partial(_attn_kernel, n_q_heads=n_q, n_kv_heads=n_kv)
    return pl.pallas_call(
        _with_casts(body, len(in_specs), c_slabs, cur, 2),
        grid=(batch, nb),
        in_specs=in_specs + c_in,
        out_specs=[pl.BlockSpec((WINDOW, dq), lambda b, i: (cur(b, i), 0))] + c_out,
        out_shape=[jax.ShapeDtypeStruct((M, dq), BF16)] + c_shape,
        scratch_shapes=[
            pltpu.VMEM((WINDOW, dq), BF16),
            pltpu.VMEM((n_kv, 2, 2 * WINDOW, L), BF16),
            pltpu.VMEM((n_kv, 2, 2 * WINDOW, 2 * L), BF16),
        ],
        compiler_params=_params(*(["arbitrary" if cast_jobs else "parallel"] * 2)),
        name="swa_attention",
    )(cos_t, sin_t, cos_t, sin_t, qkv, qkv, qkv, qkv, qkv,
      _pair_lane_gain(q_gain), _pair_lane_gain(k_gain), sinks.astype(F32), *[j[0] for j in cast_jobs])


def _hgrn_kernel(q_ref, f_ref, i_ref, g_ref, lb_ref, gain_ref, o_ref, st_ref, upd_ref, a_ref, hist_ref, *, rows):
    SUB, DK, S8 = HGRN_SUB, HGRN_HEAD_DIM, V7X_SUBLANES
    G = V7X_LANES
    nblk = rows // SUB

    @pl.when(pl.program_id(2) == 0)
    def _():
        st_ref[...] = jnp.zeros(st_ref.shape, F32)

    lb = lb_ref[...]
    one_m_lb = 1.0 - lb

    ri = lax.broadcasted_iota(jnp.int32, (2 * G, G), 0)
    ci = lax.broadcasted_iota(jnp.int32, (2 * G, G), 1)
    rr = ri % G
    same_blk = (rr // SUB) == (ci // SUB)
    take = ((ri < G) & (rr >= ci)) | ((ri >= G) & (rr < ci))
    cum_op = jnp.where(same_blk & take, 1.0, 0.0).astype(BF16)
    oi = lax.broadcasted_iota(jnp.int32, (2 * DK, 2 * DK), 0) // DK
    oj = lax.broadcasted_iota(jnp.int32, (2 * DK, 2 * DK), 1) // DK
    ones2 = jnp.where(oi == oj, 1.0, 0.0).astype(BF16)

    f = f_ref[...]
    e = jnp.exp(-jnp.abs(f))
    inv = 1.0 / (1.0 + e)
    sig_pos = jnp.where(f >= 0, inv, e * inv)
    sig_neg = jnp.where(f >= 0, e * inv, inv)
    gk2 = jnp.log2(lb + one_m_lb * sig_pos)
    kk = one_m_lb * sig_neg
    b_parts, r_parts = [], []
    for gi in range(rows // G):
        hi, lo = _split2(gk2[gi * G:(gi + 1) * G, :])
        br = _dot(cum_op, jnp.concatenate([hi, lo], axis=1))
        br = br[:, :DK] + br[:, DK:]
        b_parts.append(br[:G])
        r_parts.append(br[G:])
    b2 = jnp.concatenate(b_parts, axis=0)
    r2 = jnp.concatenate(r_parts, axis=0)
    q = _silu(q_ref[...])
    v = i_ref[...]
    q_dec = (q * jnp.exp2(b2)).astype(BF16)
    k_dec = (kk * jnp.exp2(r2)).astype(BF16)
    v_bf = v.astype(BF16)
    c2 = b2 - jnp.log2(kk)

    def halves(x):
        x4 = x.reshape(nblk, 2, S8, DK)
        return x4[:, 0], x4[:, 1]

    q_lo, q_hi = (h.reshape(nblk * S8, DK).astype(BF16) for h in halves(q))
    b_lo, b_hi = halves(b2)
    c_lo, c_hi = halves(c2)
    sub = lax.broadcasted_iota(jnp.int32, (1, S8, DK), 1)

    def weights(qq_bf, bb, cc, valid):
        ex = bb - cc
        if valid is not None:
            ex = jnp.where(valid, ex, -jnp.inf)
        return qq_bf * jnp.exp2(ex.reshape(nblk * S8, DK).astype(BF16))

    def lane_sums(slot, w0, w1):
        a = _dot(jnp.concatenate([w0, w1], axis=1), ones2)
        a_ref[slot] = a[:, :DK]
        a_ref[slot + 1] = a[:, DK:]

    carry = {"st": st_ref[...]}

    def state_slice(i):
        quarter = nblk // 4
        for n in range((i % 4) * quarter, (i % 4 + 1) * quarter):
            sl = slice(n * SUB, (n + 1) * SUB)
            if i < 4:
                upd_ref[n] = _dot_tn(v_bf[sl, :], k_dec[sl, :])
            else:
                st = carry["st"]
                hist_ref[sl, :] = _dot_nt(q_dec[sl, :], st.astype(BF16))
                carry["st"] = st * jnp.exp2(b2[n * SUB + SUB - 1:n * SUB + SUB, :]) + upd_ref[n]

    lane_sums(0, weights(q_lo, b_lo, c_lo, None), weights(q_hi, b_hi, c_hi, None))
    state_slice(0)
    rolled = [c_lo]
    for d in range(1, S8):
        m = sub >= d
        rc_lo, rc_hi = pltpu.roll(c_lo, d, 1), pltpu.roll(c_hi, d, 1)
        lane_sums(2 * d, weights(q_lo, b_lo, rc_lo, m), weights(q_hi, b_hi, jnp.where(m, rc_hi, rc_lo), None))
        rolled.append(rc_lo)
        state_slice(d)
    for d in range(0, S8, 2):
        lane_sums(2 * S8 + d, weights(q_hi, b_hi, rolled[d], (sub >= d) if d else None),
                  weights(q_hi, b_hi, rolled[d + 1], sub >= d + 1))
    st_ref[...] = carry["st"]

    CH = 8
    gain = gain_ref[...]
    for ck in range(nblk // CH):
        half_rows = slice(ck * CH * S8, (ck + 1) * CH * S8)
        full_rows = slice(ck * CH * SUB, (ck + 1) * CH * SUB)
        v4 = i_ref[full_rows, :].reshape(CH, 2, S8, DK)
        vl, vh = v4[:, 0], v4[:, 1]

        def a_of(slot):
            return a_ref[slot, half_rows, :].reshape(CH, S8, DK)

        acc_lo = a_of(0) * vl
        acc_hi = a_of(1) * vh + a_of(2 * S8) * vl
        for d in range(1, S8):
            m = sub >= d
            rl, rh = pltpu.roll(vl, d, 1), pltpu.roll(vh, d, 1)
            acc_lo = acc_lo + a_of(2 * d) * rl
            acc_hi = acc_hi + a_of(2 * d + 1) * jnp.where(m, rh, rl) + a_of(2 * S8 + d) * rl
        o = jnp.concatenate([acc_lo[:, None], acc_hi[:, None]], axis=1).reshape(CH * SUB, DK) + hist_ref[full_rows, :]
        ms = jnp.mean(o * o, axis=-1, keepdims=True)
        y = o * lax.rsqrt(ms + NORM_EPS) * gain * _silu(g_ref[full_rows, :])
        o_ref[full_rows, :] = y.astype(o_ref.dtype)


def hgrn_recurrence(qfig, lb, out_gain, *, batch, seq_len, rows=1024, cast_jobs=()):
    M = qfig.shape[0]
    D = qfig.shape[1] // 4
    dk = HGRN_HEAD_DIM
    H = D // dk
    rows = min(rows, seq_len)
    nt = seq_len // rows

    def sect(j):
        return pl.BlockSpec((rows, dk), lambda b, h, t, j=j: (b * nt + t, j * H + h))

    def flat_step(b, h, t):
        return (b * H + h) * nt + t

    in_specs = [sect(0), sect(1), sect(2), sect(3),
                pl.BlockSpec((1, dk), lambda b, h, t: (0, h)),
                pl.BlockSpec((1, dk), lambda b, h, t: (0, 0))]
    c_in, c_out, c_shape, c_slabs = _cast_specs(cast_jobs, batch * H * nt, flat_step)
    return pl.pallas_call(
        _with_casts(functools.partial(_hgrn_kernel, rows=rows), len(in_specs), c_slabs, flat_step, 3),
        grid=(batch, H, nt),
        in_specs=in_specs + c_in,
        out_specs=[pl.BlockSpec((rows, dk), lambda b, h, t: (b * nt + t, h))] + c_out,
        out_shape=[jax.ShapeDtypeStruct((M, D), BF16)] + c_shape,
        scratch_shapes=[pltpu.VMEM((dk, dk), F32), pltpu.VMEM((rows // HGRN_SUB, dk, dk), F32),
                        pltpu.VMEM((3 * V7X_SUBLANES, rows // 2, dk), F32), pltpu.VMEM((rows, dk), F32)],
        compiler_params=_params(*(["arbitrary" if cast_jobs else "parallel"] * 2), "arbitrary"),
        name="hgrn_recurrence",
    )(qfig, qfig, qfig, qfig, lb.reshape(1, D).astype(F32), out_gain.reshape(1, dk).astype(F32),
      *[j[0] for j in cast_jobs])


def kernel(x, positions, mixer_norm, mlp_norm, attn_w_qkv, attn_b_qkv, attn_q_norm, attn_k_norm, attn_sinks, attn_w_o, attn_b_o, hgrn_lower_bounds, hgrn_w_in, hgrn_out_norm, hgrn_w_o, conv_w_in, conv_w, conv_w_out, mlp_w_up, mlp_conv_w, mlp_conv_b, mlp_w_down):
    B, S, D = x.shape
    M = B * S
    depth = mixer_norm.shape[0]
    xf = x.reshape(M, D).astype(F32)
    bf = lambda w: w.astype(BF16)

    lb_soft = jax.nn.softmax(hgrn_lower_bounds.astype(F32), axis=0)
    lb_table = jnp.cumsum(lb_soft, axis=0) - lb_soft[0:1]
    cos_t, sin_t = rope_tables(positions.reshape(M, 1).astype(F32))
    dq = attn_w_o.shape[1]
    n_q = dq // ATTN_HEAD_DIM
    dk = dq + ATTN_KV_HEADS * ATTN_HEAD_DIM

    def qkv_layout(w):
        return jnp.concatenate([pair_interleave(w[..., :dq], n_q),
                                pair_interleave(w[..., dq:dk], ATTN_KV_HEADS), w[..., dk:]], axis=-1)

    stacks = {"up": mlp_w_up, "down": mlp_w_down, "cin": conv_w_in, "hin": hgrn_w_in, "hout": hgrn_w_o,
              "cout": conv_w_out, "aout": attn_w_o}
    ready = {}
    first_up = min(2, depth)

    def run_with_casts(fn, wanted):
        wanted = [w for w in wanted if w[2] > 0]
        out, *casts = fn(cast_jobs=[(stacks[name], first, count) for name, first, count in wanted])
        for (name, first, count), c in zip(wanted, casts):
            for i in range(count):
                ready[(name, first + i)] = (c, i)
        return out

    def weight(name, l, convert=True):
        if (name, l) not in ready:
            ready[(name, l)] = (bf(stacks[name][l:l + 1]), 0) if convert else (stacks[name], l)
        return ready[(name, l)]

    def out_proj(o, name, l, **kw):
        w, wl = weight(name, l, convert=False)
        return matmul(o, w, layer=wl, bn=1024 if w.dtype == BF16 else 512, **kw)

    counts = [0] * N_MIXERS
    for layer in range(depth):
        kind = layer % N_MIXERS
        j = counts[kind]
        counts[kind] += 1
        h = rmsnorm(xf, mixer_norm[layer])
        if kind == 0:
            qkv = matmul(h, bf(qkv_layout(attn_w_qkv[j])), bias=qkv_layout(attn_b_qkv[j]), bn=1024)
            o = run_with_casts(
                functools.partial(swa_attention, qkv, cos_t, sin_t, attn_q_norm[j], attn_k_norm[j], attn_sinks[j],
                                  batch=B, seq_len=S),
                [("up", 0, first_up), ("down", 0, 1), ("hin", 0, hgrn_w_in.shape[0])] if layer == 0 else [])
            xf = out_proj(o, "aout", j, bias=attn_b_o[j], resid=xf)
        elif kind == 1:
            w, wl = weight("hin", j, convert=False)
            qfig = matmul(h, w, layer=wl, bn=1024 if w.dtype == BF16 else 512)
            later = [("up", first_up, depth - first_up), ("down", layer, depth - layer), ("cin", 0, conv_w_in.shape[0]),
                     ("hout", 0, hgrn_w_o.shape[0]), ("cout", 0, conv_w_out.shape[0]),
                     ("aout", 1, attn_w_o.shape[0] - 1)]
            o = run_with_casts(
                functools.partial(hgrn_recurrence, qfig, lb_table[layer], hgrn_out_norm[j], batch=B, seq_len=S),
                later if j == 0 else [])
            xf = out_proj(o, "hout", j, resid=xf)
        else:
            w, wl = weight("cin", j)
            y = gated_conv_matmul(h, w, conv_w[j], None, layer=wl, n_branch=3, seq_len=S, bm=2048)
            xf = out_proj(y, "cout", j, resid=xf)
        h = rmsnorm(xf, mlp_norm[layer])
        w, wl = weight("up", layer)
        act = gated_conv_matmul(h, w, mlp_conv_w[layer], mlp_conv_b[layer], layer=wl, n_branch=2,
                                seq_len=S, bm=2048)
        w, wl = weight("down", layer)
        xf = matmul(act, w, layer=wl, resid=xf, bm=512, bn=512)
    return xf.reshape(B, S, D).astype(x.dtype)
```

```python
import functools
import math

import jax
import jax.numpy as jnp
from jax import lax
from jax.experimental import pallas as pl
from jax.experimental.pallas import tpu as pltpu

F32 = jnp.float32
BF16 = jnp.bfloat16

V7X_LANES = 128
V7X_SUBLANES = 8
V7X_VMEM_LIMIT_BYTES = 56 * 1024 * 1024
V7X_MXU_ROWS_PER_WEIGHT_PUSH = 128

N_MIXERS = 3
ATTN_HEAD_DIM = 64
ATTN_KV_HEADS = 8
WINDOW = 128
ROPE_THETA = 10000.0
HGRN_HEAD_DIM = 128
HGRN_SUB = 16
NORM_EPS = 1e-6
MASK_VALUE = -1e30
GLU_BLOCK = 256
LOG2E = math.log2(math.e)


def _params(*semantics):
    return pltpu.CompilerParams(dimension_semantics=semantics, vmem_limit_bytes=V7X_VMEM_LIMIT_BYTES)


def _dot(a, b):
    return jnp.dot(a, b, preferred_element_type=F32)


def _dot_nt(a, b):
    return lax.dot_general(a, b, (((1,), (1,)), ((), ())), preferred_element_type=F32)


def _dot_tn(a, b):
    return lax.dot_general(a, b, (((0,), (0,)), ((), ())), preferred_element_type=F32)


def _split2(x):
    hi = x.astype(BF16)
    lo = (x - hi.astype(F32)).astype(BF16)
    return hi, lo


def _sigmoid(x):
    return 1.0 / (1.0 + jnp.exp(-x))


def _silu(x):
    return x * _sigmoid(x)


def _with_casts(body, n_in, slab_counts, step_of, n_axes):
    n_cast = len(slab_counts)

    def kernel(*refs):
        cast_in = refs[n_in:n_in + n_cast]
        o_ref = refs[n_in + n_cast]
        cast_out = refs[n_in + n_cast + 1:n_in + 2 * n_cast + 1]
        step = step_of(*[pl.program_id(a) for a in range(n_axes)])
        for src, dst, (n_slabs, pair_width) in zip(cast_in, cast_out, slab_counts):
            @pl.when(step < n_slabs)
            def _(src=src, dst=dst, pw=pair_width):
                if pw is None:
                    dst[...] = src[...].astype(BF16)
                else:
                    half = src.shape[-1] // 2
                    for n in range(half // pw):
                        dst[:, 2 * n * pw:(2 * n + 1) * pw] = src[:, n * pw:(n + 1) * pw].astype(BF16)
                        dst[:, (2 * n + 1) * pw:(2 * n + 2) * pw] = src[:, half + n * pw:half + (n + 1) * pw].astype(BF16)
        body(*refs[:n_in], o_ref, *refs[n_in + 2 * n_cast + 1:])
    return kernel


def _cast_specs(jobs, n_steps, step_of):
    in_specs, out_specs, out_shapes, slab_counts = [], [], [], []
    bf16_rows = 2 * V7X_SUBLANES
    for arr, first, count, pair_width in jobs:
        _, rows, cols = arr.shape
        r = next(r for r in range(bf16_rows, rows + 1, bf16_rows)
                 if rows % r == 0 and count * (rows // r) <= n_steps)
        per_layer = rows // r

        def slab(*g, per_layer=per_layer, last=count * per_layer - 1):
            return jnp.minimum(step_of(*g), last)

        in_specs.append(pl.BlockSpec(
            (None, r, cols), lambda *g, s=slab, p=per_layer, first=first: (first + s(*g) // p, s(*g) % p, 0)))
        out_specs.append(pl.BlockSpec((None, r, cols), lambda *g, s=slab, p=per_layer: (s(*g) // p, s(*g) % p, 0)))
        out_shapes.append(jax.ShapeDtypeStruct((count, rows, cols), BF16))
        slab_counts.append((count * per_layer, pair_width))
    return in_specs, out_specs, out_shapes, slab_counts


def _rmsnorm_kernel(x_ref, g_ref, o_ref, *, chunk):
    g = g_ref[...]
    for c in range(x_ref.shape[0] // chunk):
        rows = slice(c * chunk, (c + 1) * chunk)
        x = x_ref[rows, :]
        ms = jnp.mean(x * x, axis=-1, keepdims=True)
        o_ref[rows, :] = (x * lax.rsqrt(ms + NORM_EPS) * g).astype(o_ref.dtype)


def rmsnorm(x, gain, *, bm=1024, chunk=256):
    M, D = x.shape
    bm = min(bm, M)
    return pl.pallas_call(
        functools.partial(_rmsnorm_kernel, chunk=min(chunk, bm)),
        grid=(M // bm,),
        in_specs=[pl.BlockSpec((bm, D), lambda i: (i, 0)), pl.BlockSpec((1, D), lambda i: (0, 0))],
        out_specs=pl.BlockSpec((bm, D), lambda i: (i, 0)),
        out_shape=jax.ShapeDtypeStruct((M, D), BF16),
        compiler_params=_params("parallel"),
        name="rmsnorm",
    )(x, gain.reshape(1, D).astype(F32))


def _matmul_kernel(*refs, has_bias, has_resid, cast_w):
    x_ref, w_ref = refs[0], refs[1]
    if cast_w:
        o_ref, w_bf = refs[-2], refs[-1]

        @pl.when(pl.program_id(1) == 0)
        def _():
            w_bf[...] = w_ref[...].astype(BF16)

        w = w_bf[...]
    else:
        o_ref = refs[-1]
        w = w_ref[...]
    acc = _dot(x_ref[...], w)
    k = 2
    if has_bias:
        acc = acc + refs[k][...]
        k += 1
    if has_resid:
        acc = acc + refs[k][...]
    o_ref[...] = acc.astype(o_ref.dtype)


def matmul(x, w, *, layer=None, bias=None, resid=None, out_dtype=F32, bm=1024, bn=512):
    M, K = x.shape
    N = w.shape[-1]
    bm, bn = min(bm, M), min(bn, N)
    cast_w = w.dtype != BF16
    if layer is None:
        w_spec = pl.BlockSpec((K, bn), lambda n, m: (0, n))
    else:
        w_spec = pl.BlockSpec((None, K, bn), lambda n, m: (layer, 0, n))
    in_specs = [pl.BlockSpec((bm, K), lambda n, m: (m, 0)), w_spec]
    args = [x, w]
    if bias is not None:
        in_specs.append(pl.BlockSpec((1, bn), lambda n, m: (0, n)))
        args.append(bias.reshape(1, N).astype(F32))
    if resid is not None:
        in_specs.append(pl.BlockSpec((bm, bn), lambda n, m: (m, n)))
        args.append(resid)
    return pl.pallas_call(
        functools.partial(_matmul_kernel, has_bias=bias is not None, has_resid=resid is not None, cast_w=cast_w),
        grid=(N // bn, M // bm),
        in_specs=in_specs,
        out_specs=pl.BlockSpec((bm, bn), lambda n, m: (m, n)),
        out_shape=jax.ShapeDtypeStruct((M, N), out_dtype),
        scratch_shapes=[pltpu.VMEM((K, bn), BF16)] if cast_w else [],
        compiler_params=_params("parallel", "arbitrary" if cast_w else "parallel"),
        name="matmul",
    )(*args)


def _gated_conv_kernel(*refs, n_branch, n_w, bm, n_mt, tiles_per_seq):
    x_ref = refs[0]
    w_refs = refs[1:1 + n_w]
    cw_ref, cb_ref, o_ref = refs[1 + n_w:4 + n_w]
    scratch = refs[4 + n_w:]
    set_a, set_b, zbuf = scratch[0:2], scratch[2:4], scratch[4]
    S8 = V7X_SUBLANES
    t = pl.program_id(0)

    @pl.when(t == 0)
    def _():
        for buf in scratch:
            buf[...] = jnp.zeros(buf.shape, F32)

    def step(src, dst):
        m_prev = jnp.maximum(t - 1, 0) % n_mt
        seq_start = (m_prev % tiles_per_seq) == 0
        tail = jnp.where(seq_start, 0.0, zbuf[...])
        cw = cw_ref[...]
        sub = V7X_MXU_ROWS_PER_WEIGHT_PUSH
        row8 = lax.broadcasted_iota(jnp.int32, (S8, 1), 0)

        def shifted(z, tail, k):
            s = pltpu.roll(z, k, axis=0)
            head = jnp.where(row8 < k, pltpu.roll(tail, k, axis=0), s[0:S8, :])
            return jnp.concatenate([head, s[S8:, :]], axis=0)

        for i in range(bm // sub):
            r = slice(i * sub, (i + 1) * sub)
            xs = x_ref[r, :]
            if n_w == n_branch:
                ys = [_dot(xs, w[...]) for w in w_refs]
            else:
                y2 = _dot(xs, w_refs[0][...])
                ys = [y2[:, :y2.shape[1] // 2], y2[:, y2.shape[1] // 2:]]
            dst[0][r, :], dst[1][r, :] = (ys[0], ys[1]) if n_branch == 2 else (ys[1] * ys[2], ys[0])
            z, gate = src[0][r, :], src[1][r, :]
            conv = cw[0:1, :] * shifted(z, tail, 2) + cw[1:2, :] * shifted(z, tail, 1) + cw[2:3, :] * z
            tail = z[sub - S8:sub, :]
            if n_branch == 2:
                out = _silu(conv + cb_ref[...]) * gate
            else:
                out = gate * conv
            o_ref[r, :] = out.astype(o_ref.dtype)
        zbuf[...] = tail

    @pl.when(t % 2 == 0)
    def _():
        step(set_b, set_a)

    @pl.when(t % 2 == 1)
    def _():
        step(set_a, set_b)


def gated_conv_matmul(x, w, conv_w, conv_b, *, layer, n_branch, seq_len, bm=1024, bn=256, paired=False):
    M, K = x.shape
    N = w.shape[-1] // n_branch
    bm, bn = min(bm, seq_len), min(bn, N)
    nb, n_mt = N // bn, M // bm
    last = nb * n_mt - 1

    def mm_tile(t):
        return jnp.minimum(t, last)

    def ep_tile(t):
        return jnp.maximum(t - 1, 0)

    in_specs = [pl.BlockSpec((bm, K), lambda t: (mm_tile(t) % n_mt, 0))]
    assert not paired or n_branch == 2
    n_w = 1 if paired else n_branch
    if paired:
        in_specs.append(pl.BlockSpec((None, K, 2 * bn), lambda t: (layer, 0, mm_tile(t) // n_mt)))
    for j in range(0 if paired else n_branch):
        in_specs.append(pl.BlockSpec((None, K, bn), lambda t, j=j: (layer, 0, mm_tile(t) // n_mt + j * nb)))
    in_specs.append(pl.BlockSpec((3, bn), lambda t: (0, ep_tile(t) // n_mt)))
    in_specs.append(pl.BlockSpec((1, bn), lambda t: (0, ep_tile(t) // n_mt)))
    if conv_b is None:
        conv_b = jnp.zeros((N,), F32)
    return pl.pallas_call(
        functools.partial(_gated_conv_kernel, n_branch=n_branch, n_w=n_w, bm=bm, n_mt=n_mt,
                          tiles_per_seq=seq_len // bm),
        grid=(nb * n_mt + 1,),
        in_specs=in_specs,
        out_specs=pl.BlockSpec((bm, bn), lambda t: (ep_tile(t) % n_mt, ep_tile(t) // n_mt)),
        out_shape=jax.ShapeDtypeStruct((M, N), BF16),
        scratch_shapes=[pltpu.VMEM((bm, bn), F32)] * 4 + [pltpu.VMEM((V7X_SUBLANES, bn), F32)],
        compiler_params=_params("arbitrary"),
        name="gated_conv_matmul",
    )(x, *([w] * n_w), conv_w.astype(F32), conv_b.reshape(1, N).astype(F32))


def pair_interleave(w, n_heads):
    half = ATTN_HEAD_DIM // 2
    lead = w.shape[:-1]
    w = w.reshape(*lead, n_heads // 2, 2, 2, half)
    return jnp.swapaxes(w, -3, -2).reshape(*lead, n_heads * ATTN_HEAD_DIM)


def _pair_lane_gain(gain):
    half = ATTN_HEAD_DIM // 2
    g = gain.astype(F32)
    return jnp.concatenate([g[:half], g[:half], g[half:], g[half:]]).reshape(1, V7X_LANES)


def _rope_table_kernel(pos_ref, invf_ref, sgn_ref, cos_ref, sin_ref):
    ang = pos_ref[...] * invf_ref[...]
    cos_ref[...] = jnp.cos(ang)
    sin_ref[...] = jnp.sin(ang) * sgn_ref[...]


def rope_tables(pos, *, bm=512):
    M = pos.shape[0]
    L = V7X_LANES
    half = ATTN_HEAD_DIM // 2
    bm = min(bm, M)
    lane = jnp.arange(L)
    inv_freq = (ROPE_THETA ** (-(lane % half).astype(F32) / half)).reshape(1, L)
    sgn = jnp.where(lane < L // 2, -1.0, 1.0).astype(F32).reshape(1, L)
    row = pl.BlockSpec((1, L), lambda i: (0, 0))
    blk = pl.BlockSpec((bm, L), lambda i: (i, 0))
    return pl.pallas_call(
        _rope_table_kernel,
        grid=(M // bm,),
        in_specs=[pl.BlockSpec((bm, 1), lambda i: (i, 0)), row, row],
        out_specs=[blk, blk],
        out_shape=[jax.ShapeDtypeStruct((M, L), F32)] * 2,
        compiler_params=_params("parallel"),
        name="rope_tables",
    )(pos, inv_freq, sgn)


def _attn_kernel(cos_c_ref, sin_c_ref, cos_p_ref, sin_p_ref, q_ref, kc_ref, kp_ref, vc_ref, vp_ref,
                 qg_ref, kg_ref, sink_ref, o_ref, qs, km, vm, *, n_q_heads, n_kv_heads):
    hd = ATTN_HEAD_DIM
    L = V7X_LANES
    W = WINDOW
    blk = pl.program_id(1)
    group = n_q_heads // n_kv_heads
    pairs_per_kv = group // 2

    lane = lax.broadcasted_iota(jnp.int32, (1, L), 1)
    even_lanes = (lane % hd) < (hd // 2)
    low_lanes = lane < hd
    hi_ = lax.broadcasted_iota(jnp.int32, (L, L), 0)
    hj_ = lax.broadcasted_iota(jnp.int32, (L, L), 1)
    same_head = ((hi_ % hd) // (hd // 2)) == ((hj_ % hd) // (hd // 2))
    head_mean = jnp.where(same_head, 1.0 / hd, 0.0).astype(BF16)

    def norm_rope(x, cg, sg):
        ms = _dot((x * x).astype(BF16), head_mean)
        rs = lax.rsqrt(ms + NORM_EPS)
        return rs * (x * cg + pltpu.roll(x, hd, axis=1) * sg)

    qscale = (hd ** -0.5) * LOG2E
    qg = qg_ref[...] * qscale
    kg = kg_ref[...]
    qg_r = pltpu.roll(qg, hd, axis=1)
    kg_r = pltpu.roll(kg, hd, axis=1)
    cos_c, sin_c = cos_c_ref[...], sin_c_ref[...]
    cos_p, sin_p = cos_p_ref[...], sin_p_ref[...]
    cq, sq = cos_c * qg, sin_c * qg_r
    ckc, skc = cos_c * kg, sin_c * kg_r
    ckp, skp = cos_p * kg, sin_p * kg_r

    for c in range(n_q_heads * hd // L):
        sl = slice(c * L, (c + 1) * L)
        qs[:, sl] = norm_rope(q_ref[:, sl], cq, sq).astype(BF16)

    for c in range(n_kv_heads * hd // L):
        sl = slice(c * L, (c + 1) * L)
        kk = jnp.concatenate([norm_rope(kp_ref[:, sl], ckp, skp), norm_rope(kc_ref[:, sl], ckc, skc)], axis=0)
        vv = jnp.concatenate([vp_ref[:, sl], vc_ref[:, sl]], axis=0)
        k_dn = pltpu.roll(kk, hd // 2, axis=1)
        k_up = pltpu.roll(kk, L - hd // 2, axis=1)
        v_sw = pltpu.roll(vv, hd, axis=1)
        a, b = 2 * c, 2 * c + 1
        km[a, 0] = jnp.where(even_lanes, kk, 0.0).astype(BF16)
        km[a, 1] = jnp.where(even_lanes, 0.0, k_dn).astype(BF16)
        km[b, 0] = jnp.where(even_lanes, k_up, 0.0).astype(BF16)
        km[b, 1] = jnp.where(even_lanes, 0.0, kk).astype(BF16)
        vm[a, 0, :, 0:L] = jnp.where(low_lanes, vv, 0.0).astype(BF16)
        vm[a, 1, :, 0:L] = jnp.where(low_lanes, 0.0, v_sw).astype(BF16)
        vm[b, 0, :, 0:L] = jnp.where(low_lanes, v_sw, 0.0).astype(BF16)
        vm[b, 1, :, 0:L] = jnp.where(low_lanes, 0.0, vv).astype(BF16)
        for h in (a, b):
            for par in range(2):
                vm[h, par, :, L:2 * L] = jnp.ones((2 * W, L), BF16)

    ri = lax.broadcasted_iota(jnp.int32, (W, W), 0)
    ci = lax.broadcasted_iota(jnp.int32, (W, W), 1)
    use_cur = ci <= ri
    prev_bias = jnp.where(blk > 0, 0.0, MASK_VALUE).astype(F32)

    for hk in range(n_kv_heads):
        heads = [(p, par) for p in range(pairs_per_kv) for par in range(2)]
        sinks = [sink_ref[hk * group + 2 * p + par] * LOG2E for p, par in heads]
        scores = []
        for p, par in heads:
            c = hk * pairs_per_kv + p
            s2 = _dot_nt(qs[:, c * L:(c + 1) * L], km[hk, par])
            scores.append(jnp.where(use_cur, s2[:, W:], s2[:, :W] + prev_bias))
        maxes = [jnp.maximum(jnp.max(s, axis=1, keepdims=True), sink) for s, sink in zip(scores, sinks)]
        probs = [jnp.exp2(s - mx) for s, mx in zip(scores, maxes)]
        outs = []
        for (p, par), pr in zip(heads, probs):
            p2 = jnp.concatenate([jnp.where(use_cur, 0.0, pr), jnp.where(use_cur, pr, 0.0)], axis=1)
            outs.append(_dot(p2.astype(BF16), vm[hk, par]))
        for p in range(pairs_per_kv):
            c = hk * pairs_per_kv + p
            (o_a, o_b), (mx_a, mx_b), (sk_a, sk_b) = outs[2 * p:2 * p + 2], maxes[2 * p:2 * p + 2], sinks[2 * p:2 * p + 2]
            den = jnp.where(low_lanes, o_a[:, L:] + jnp.exp2(sk_a - mx_a), o_b[:, L:] + jnp.exp2(sk_b - mx_b))
            o_ref[:, c * L:(c + 1) * L] = ((o_a[:, :L] + o_b[:, :L]) / den).astype(o_ref.dtype)


def swa_attention(qkv, cos_t, sin_t, q_gain, k_gain, sinks, *, batch, seq_len, cast_jobs=()):
    M = qkv.shape[0]
    hd = ATTN_HEAD_DIM
    n_kv = ATTN_KV_HEADS
    n_q = (qkv.shape[1] - 2 * n_kv * hd) // hd
    dq, dkv = n_q * hd, n_kv * hd
    nb = seq_len // WINDOW
    L = V7X_LANES
    kcol, vcol = dq // dkv, dq // dkv + 1

    def cur(b, i):
        return b * nb + i

    def prev(b, i):
        return b * nb + jnp.maximum(i - 1, 0)

    row = pl.BlockSpec((1, L), lambda b, i: (0, 0))
    in_specs = [
        pl.BlockSpec((WINDOW, L), lambda b, i: (cur(b, i), 0)),
        pl.BlockSpec((WINDOW, L), lambda b, i: (cur(b, i), 0)),
        pl.BlockSpec((WINDOW, L), lambda b, i: (prev(b, i), 0)),
        pl.BlockSpec((WINDOW, L), lambda b, i: (prev(b, i), 0)),
        pl.BlockSpec((WINDOW, dq), lambda b, i: (cur(b, i), 0)),
        pl.BlockSpec((WINDOW, dkv), lambda b, i: (cur(b, i), kcol)),
        pl.BlockSpec((WINDOW, dkv), lambda b, i: (prev(b, i), kcol)),
        pl.BlockSpec((WINDOW, dkv), lambda b, i: (cur(b, i), vcol)),
        pl.BlockSpec((WINDOW, dkv), lambda b, i: (prev(b, i), vcol)),
        row, row,
        pl.BlockSpec(memory_space=pltpu.SMEM),
    ]
    c_in, c_out, c_shape, c_slabs = _cast_specs(cast_jobs, batch * nb, cur)
    body = functools.partial(_attn_kernel, n_q_heads=n_q, n_kv_heads=n_kv)
    return pl.pallas_call(
        _with_casts(body, len(in_specs), c_slabs, cur, 2),
        grid=(batch, nb),
        in_specs=in_specs + c_in,
        out_specs=[pl.BlockSpec((WINDOW, dq), lambda b, i: (cur(b, i), 0))] + c_out,
        out_shape=[jax.ShapeDtypeStruct((M, dq), BF16)] + c_shape,
        scratch_shapes=[
            pltpu.VMEM((WINDOW, dq), BF16),
            pltpu.VMEM((n_kv, 2, 2 * WINDOW, L), BF16),
            pltpu.VMEM((n_kv, 2, 2 * WINDOW, 2 * L), BF16),
        ],
        compiler_params=_params(*(["arbitrary" if cast_jobs else "parallel"] * 2)),
        name="swa_attention",
    )(cos_t, sin_t, cos_t, sin_t, qkv, qkv, qkv, qkv, qkv,
      _pair_lane_gain(q_gain), _pair_lane_gain(k_gain), sinks.astype(F32), *[j[0] for j in cast_jobs])


def _hgrn_kernel(q_ref, f_ref, i_ref, g_ref, lb_ref, gain_ref, o_ref, st_ref, upd_ref, a_ref, hist_ref, *, rows):
    SUB, DK, S8 = HGRN_SUB, HGRN_HEAD_DIM, V7X_SUBLANES
    G = V7X_LANES
    nblk = rows // SUB

    @pl.when(pl.program_id(2) == 0)
    def _():
        st_ref[...] = jnp.zeros(st_ref.shape, F32)

    lb = lb_ref[...]
    one_m_lb = 1.0 - lb

    ri = lax.broadcasted_iota(jnp.int32, (2 * G, G), 0)
    ci = lax.broadcasted_iota(jnp.int32, (2 * G, G), 1)
    rr = ri % G
    same_blk = (rr // SUB) == (ci // SUB)
    take = ((ri < G) & (rr >= ci)) | ((ri >= G) & (rr < ci))
    cum_op = jnp.where(same_blk & take, 1.0, 0.0).astype(BF16)
    oi = lax.broadcasted_iota(jnp.int32, (2 * DK, 2 * DK), 0) // DK
    oj = lax.broadcasted_iota(jnp.int32, (2 * DK, 2 * DK), 1) // DK
    ones2 = jnp.where(oi == oj, 1.0, 0.0).astype(BF16)

    f = f_ref[...]
    e = jnp.exp(-jnp.abs(f))
    inv = 1.0 / (1.0 + e)
    sig_pos = jnp.where(f >= 0, inv, e * inv)
    sig_neg = jnp.where(f >= 0, e * inv, inv)
    gk2 = jnp.log2(lb + one_m_lb * sig_pos)
    kk = one_m_lb * sig_neg
    b_parts, r_parts = [], []
    for gi in range(rows // G):
        hi, lo = _split2(gk2[gi * G:(gi + 1) * G, :])
        br = _dot(cum_op, jnp.concatenate([hi, lo], axis=1))
        br = br[:, :DK] + br[:, DK:]
        b_parts.append(br[:G])
        r_parts.append(br[G:])
    b2 = jnp.concatenate(b_parts, axis=0)
    r2 = jnp.concatenate(r_parts, axis=0)
    q = _silu(q_ref[...])
    v = i_ref[...]
    q_dec = (q * jnp.exp2(b2)).astype(BF16)
    k_dec = (kk * jnp.exp2(r2)).astype(BF16)
    v_bf = v.astype(BF16)
    c2 = b2 - jnp.log2(kk)

    def halves(x):
        x4 = x.reshape(nblk, 2, S8, DK)
        return x4[:, 0], x4[:, 1]

    q_lo, q_hi = (h.reshape(nblk * S8, DK).astype(BF16) for h in halves(q))
    b_lo, b_hi = halves(b2)
    c_lo, c_hi = halves(c2)
    sub = lax.broadcasted_iota(jnp.int32, (1, S8, DK), 1)

    def weights(qq_bf, bb, cc, valid):
        ex = bb - cc
        if valid is not None:
            ex = jnp.where(valid, ex, -jnp.inf)
        return qq_bf * jnp.exp2(ex.reshape(nblk * S8, DK).astype(BF16))

    def lane_sums(slot, w0, w1):
        a = _dot(jnp.concatenate([w0, w1], axis=1), ones2)
        a_ref[slot] = a[:, :DK]
        a_ref[slot + 1] = a[:, DK:]

    carry = {"st": st_ref[...]}

    def state_slice(i):
        quarter = nblk // 4
        for n in range((i % 4) * quarter, (i % 4 + 1) * quarter):
            sl = slice(n * SUB, (n + 1) * SUB)
            if i < 4:
                upd_ref[n] = _dot_tn(v_bf[sl, :], k_dec[sl, :])
            else:
                st = carry["st"]
                hist_ref[sl, :] = _dot_nt(q_dec[sl, :], st.astype(BF16))
                carry["st"] = st * jnp.exp2(b2[n * SUB + SUB - 1:n * SUB + SUB, :]) + upd_ref[n]

    lane_sums(0, weights(q_lo, b_lo, c_lo, None), weights(q_hi, b_hi, c_hi, None))
    state_slice(0)
    rolled = [c_lo]
    for d in range(1, S8):
        m = sub >= d
        rc_lo, rc_hi = pltpu.roll(c_lo, d, 1), pltpu.roll(c_hi, d, 1)
        lane_sums(2 * d, weights(q_lo, b_lo, rc_lo, m), weights(q_hi, b_hi, jnp.where(m, rc_hi, rc_lo), None))
        rolled.append(rc_lo)
        state_slice(d)
    for d in range(0, S8, 2):
        lane_sums(2 * S8 + d, weights(q_hi, b_hi, rolled[d], (sub >= d) if d else None),
                  weights(q_hi, b_hi, rolled[d + 1], sub >= d + 1))
    st_ref[...] = carry["st"]

    CH = 8
    gain = gain_ref[...]
    for ck in range(nblk // CH):
        half_rows = slice(ck * CH * S8, (ck + 1) * CH * S8)
        full_rows = slice(ck * CH * SUB, (ck + 1) * CH * SUB)
        v4 = i_ref[full_rows, :].reshape(CH, 2, S8, DK)
        vl, vh = v4[:, 0], v4[:, 1]

        def a_of(slot):
            return a_ref[slot, half_rows, :].reshape(CH, S8, DK)

        acc_lo = a_of(0) * vl
        acc_hi = a_of(1) * vh + a_of(2 * S8) * vl
        for d in range(1, S8):
            m = sub >= d
            rl, rh = pltpu.roll(vl, d, 1), pltpu.roll(vh, d, 1)
            acc_lo = acc_lo + a_of(2 * d) * rl
            acc_hi = acc_hi + a_of(2 * d + 1) * jnp.where(m, rh, rl) + a_of(2 * S8 + d) * rl
        o = jnp.concatenate([acc_lo[:, None], acc_hi[:, None]], axis=1).reshape(CH * SUB, DK) + hist_ref[full_rows, :]
        ms = jnp.mean(o * o, axis=-1, keepdims=True)
        y = o * lax.rsqrt(ms + NORM_EPS) * gain * _silu(g_ref[full_rows, :])
        o_ref[full_rows, :] = y.astype(o_ref.dtype)


def hgrn_recurrence(qfig, lb, out_gain, *, batch, seq_len, rows=1024, cast_jobs=()):
    M = qfig.shape[0]
    D = qfig.shape[1] // 4
    dk = HGRN_HEAD_DIM
    H = D // dk
    rows = min(rows, seq_len)
    nt = seq_len // rows

    def sect(j):
        return pl.BlockSpec((rows, dk), lambda b, h, t, j=j: (b * nt + t, j * H + h))

    def flat_step(b, h, t):
        return (b * H + h) * nt + t

    in_specs = [sect(0), sect(1), sect(2), sect(3),
                pl.BlockSpec((1, dk), lambda b, h, t: (0, h)),
                pl.BlockSpec((1, dk), lambda b, h, t: (0, 0))]
    c_in, c_out, c_shape, c_slabs = _cast_specs(cast_jobs, batch * H * nt, flat_step)
    return pl.pallas_call(
        _with_casts(functools.partial(_hgrn_kernel, rows=rows), len(in_specs), c_slabs, flat_step, 3),
        grid=(batch, H, nt),
        in_specs=in_specs + c_in,
        out_specs=[pl.BlockSpec((rows, dk), lambda b, h, t: (b * nt + t, h))] + c_out,
        out_shape=[jax.ShapeDtypeStruct((M, D), BF16)] + c_shape,
        scratch_shapes=[pltpu.VMEM((dk, dk), F32), pltpu.VMEM((rows // HGRN_SUB, dk, dk), F32),
                        pltpu.VMEM((3 * V7X_SUBLANES, rows // 2, dk), F32), pltpu.VMEM((rows, dk), F32)],
        compiler_params=_params(*(["arbitrary" if cast_jobs else "parallel"] * 2), "arbitrary"),
        name="hgrn_recurrence",
    )(qfig, qfig, qfig, qfig, lb.reshape(1, D).astype(F32), out_gain.reshape(1, dk).astype(F32),
      *[j[0] for j in cast_jobs])


def kernel(x, positions, mixer_norm, mlp_norm, attn_w_qkv, attn_b_qkv, attn_q_norm, attn_k_norm, attn_sinks, attn_w_o, attn_b_o, hgrn_lower_bounds, hgrn_w_in, hgrn_out_norm, hgrn_w_o, conv_w_in, conv_w, conv_w_out, mlp_w_up, mlp_conv_w, mlp_conv_b, mlp_w_down):
    B, S, D = x.shape
    M = B * S
    depth = mixer_norm.shape[0]
    xf = x.reshape(M, D).astype(F32)
    bf = lambda w: w.astype(BF16)

    lb_soft = jax.nn.softmax(hgrn_lower_bounds.astype(F32), axis=0)
    lb_table = jnp.cumsum(lb_soft, axis=0) - lb_soft[0:1]
    cos_t, sin_t = rope_tables(positions.reshape(M, 1).astype(F32))
    dq = attn_w_o.shape[1]
    n_q = dq // ATTN_HEAD_DIM
    dk = dq + ATTN_KV_HEADS * ATTN_HEAD_DIM

    def qkv_layout(w):
        return jnp.concatenate([pair_interleave(w[..., :dq], n_q),
                                pair_interleave(w[..., dq:dk], ATTN_KV_HEADS), w[..., dk:]], axis=-1)

    stacks = {"up": mlp_w_up, "down": mlp_w_down, "cin": conv_w_in, "hin": hgrn_w_in, "hout": hgrn_w_o,
              "cout": conv_w_out, "aout": attn_w_o}
    ready = {}
    first_up = min(2, depth)

    def run_with_casts(fn, wanted):
        wanted = [w for w in wanted if w[2] > 0]
        out, *casts = fn(cast_jobs=[(stacks[name], first, count, GLU_BLOCK if name == "up" else None)
                                    for name, first, count in wanted])
        for (name, first, count), c in zip(wanted, casts):
            for i in range(count):
                ready[(name, first + i)] = (c, i)
        return out

    def weight(name, l, convert=True):
        if (name, l) not in ready:
            w = stacks[name][l:l + 1]
            if name == "up":
                k, n2 = w.shape[1:]
                w = jnp.swapaxes(w.reshape(1, k, 2, n2 // (2 * GLU_BLOCK), GLU_BLOCK), 2, 3).reshape(1, k, n2)
            ready[(name, l)] = (bf(w), 0) if convert else (stacks[name], l)
        return ready[(name, l)]

    def out_proj(o, name, l, **kw):
        w, wl = weight(name, l, convert=False)
        return matmul(o, w, layer=wl, bn=1024 if w.dtype == BF16 else 512, **kw)

    counts = [0] * N_MIXERS
    for layer in range(depth):
        kind = layer % N_MIXERS
        j = counts[kind]
        counts[kind] += 1
        h = rmsnorm(xf, mixer_norm[layer])
        if kind == 0:
            qkv = matmul(h, bf(qkv_layout(attn_w_qkv[j])), bias=qkv_layout(attn_b_qkv[j]), bn=1024)
            o = run_with_casts(
                functools.partial(swa_attention, qkv, cos_t, sin_t, attn_q_norm[j], attn_k_norm[j], attn_sinks[j],
                                  batch=B, seq_len=S),
                [("up", 0, first_up), ("down", 0, 1), ("hin", 0, hgrn_w_in.shape[0])] if layer == 0 else [])
            xf = out_proj(o, "aout", j, bias=attn_b_o[j], resid=xf)
        elif kind == 1:
            w, wl = weight("hin", j, convert=False)
            qfig = matmul(h, w, layer=wl, bn=1024 if w.dtype == BF16 else 512)
            later = [("up", first_up, depth - first_up), ("down", layer, depth - layer), ("cin", 0, conv_w_in.shape[0]),
                     ("hout", 0, hgrn_w_o.shape[0]), ("cout", 0, conv_w_out.shape[0]),
                     ("aout", 1, attn_w_o.shape[0] - 1)]
            o = run_with_casts(
                functools.partial(hgrn_recurrence, qfig, lb_table[layer], hgrn_out_norm[j], batch=B, seq_len=S),
                later if j == 0 else [])
            xf = out_proj(o, "hout", j, resid=xf)
        else:
            w, wl = weight("cin", j)
            y = gated_conv_matmul(h, w, conv_w[j], None, layer=wl, n_branch=3, seq_len=S, bm=2048)
            xf = out_proj(y, "cout", j, resid=xf)
        h = rmsnorm(xf, mlp_norm[layer])
        w, wl = weight("up", layer)
        act = gated_conv_matmul(h, w, mlp_conv_w[layer], mlp_conv_b[layer], layer=wl, n_branch=2,
                                seq_len=S, bm=2048, bn=GLU_BLOCK, paired=True)
        w, wl = weight("down", layer)
        xf = matmul(act, w, layer=wl, resid=xf, bm=512, bn=512)
    return xf.reshape(B, S, D).astype(x.dtype)
```
